```python
import numpy as np
import jax
import jax.numpy as jnp
from jax import lax

D_MODEL = 1024
BATCH = 16
SEQ = 2048
DEPTH = 2

HEAD_DIM = 64
ROPE_THETA = 10000.0
NORM_EPS = 1e-6
TINY = 1e-30
N_BRANCH = 4

MOBA_HEADS = 8
MOBA_BLOCK = 256
MOBA_TOPK = 3
MOBA_QCHUNK = 8

SWA_HEADS = 8
SWA_KV_HEADS = 2
SWA_WINDOW = 128
SWA_QBLOCK = 128

RET_HEADS = 4
RET_QK_DIM = 64
RET_V_DIM = 128
RET_CHUNK = 128

NSA_HEADS = 8
NSA_KV_HEADS = 2
NSA_CMP_LEN = 32
NSA_CMP_STRIDE = 16
NSA_CMP_HIDDEN = 256
NSA_SLC_BLOCK = 64
NSA_SLC_TOPN = 16
NSA_WINDOW = 512
NSA_QCHUNK = 16
NSA_WIN_QBLOCK = 128

MOBA_W = MOBA_HEADS * HEAD_DIM
SWA_W = SWA_HEADS * HEAD_DIM
SWA_KV_W = SWA_KV_HEADS * HEAD_DIM
RET_QK_W = RET_HEADS * RET_QK_DIM
RET_W = RET_HEADS * RET_V_DIM
NSA_W = NSA_HEADS * HEAD_DIM
NSA_KV_W = NSA_KV_HEADS * HEAD_DIM
BRANCH_WIDTHS = (MOBA_W, SWA_W, RET_W, NSA_W)
D_BRANCH = MOBA_W + SWA_W + RET_W + NSA_W

IN_SPLITS = (
    ('moba_q', MOBA_W), ('moba_k', MOBA_W), ('moba_v', MOBA_W), ('moba_z', MOBA_W),
    ('swa_q', SWA_W), ('swa_k', SWA_KV_W), ('swa_v', SWA_KV_W), ('swa_z', SWA_W),
    ('ret_q', RET_QK_W), ('ret_k', RET_QK_W), ('ret_v', RET_W), ('ret_z', RET_W),
    ('nsa_q', NSA_W), ('nsa_kc', NSA_KV_W), ('nsa_vc', NSA_KV_W),
    ('nsa_ks', NSA_KV_W), ('nsa_vs', NSA_KV_W), ('nsa_kw', NSA_KV_W), ('nsa_vw', NSA_KV_W),
    ('nsa_g', 3 * NSA_HEADS), ('nsa_z', NSA_W),
    ('merge_g', N_BRANCH * D_MODEL),
)
D_IN = sum(w for _, w in IN_SPLITS)

kernel_name = 'hybrid_gated_moba_swa_retention_nsa'


def rms_norm(x, g):
    xf = x.astype(jnp.float32)
    y = xf * lax.rsqrt(jnp.mean(xf * xf, axis=-1, keepdims=True) + NORM_EPS)
    return (y * g.astype(jnp.float32)).astype(x.dtype)


def rope_tables(seq):
    inv = 1.0 / (ROPE_THETA ** (jnp.arange(0, HEAD_DIM, 2, dtype=jnp.float32) / HEAD_DIM))
    ang = jnp.arange(seq, dtype=jnp.float32)[:, None] * inv[None, :]
    return jnp.cos(ang), jnp.sin(ang)


def apply_rope(t, cos, sin):
    half = t.shape[-1] // 2
    t1 = t[..., :half].astype(jnp.float32)
    t2 = t[..., half:].astype(jnp.float32)
    return jnp.concatenate([t1 * cos - t2 * sin, t2 * cos + t1 * sin], axis=-1).astype(t.dtype)


def split_cols(h):
    cols = {}
    off = 0
    for name, width in IN_SPLITS:
        cols[name] = h[..., off:off + width]
        off += width
    return cols


def to_heads(t, n):
    b, s, _ = t.shape
    return t.reshape(b, s, n, -1).transpose(0, 2, 1, 3)


def to_gqa(t, n_kv, n_grp):
    b, s, _ = t.shape
    return t.reshape(b, s, n_kv, n_grp, -1).transpose(0, 2, 3, 1, 4)


def from_heads(o):
    b, h, s, d = o.shape
    return o.transpose(0, 2, 1, 3).reshape(b, s, h * d)


def from_gqa(o):
    b, hk, g, s, d = o.shape
    return o.transpose(0, 3, 1, 2, 4).reshape(b, s, hk * g * d)


def masked_softmax(s, mask, sink=None):
    s = jnp.where(mask, s, -jnp.inf)
    m = jnp.max(s, axis=-1, keepdims=True)
    if sink is not None:
        m = jnp.maximum(m, sink)
    m = jnp.where(jnp.isfinite(m), m, 0.0)
    e = jnp.exp(s - m)
    den = jnp.sum(e, axis=-1, keepdims=True)
    if sink is not None:
        den = den + jnp.exp(sink - m)
    return e / jnp.maximum(den, TINY)


def banded_attention(q, k, v, window, qblock, sink=None):
    b, hk, g, s, dh = q.shape
    scale = dh ** -0.5
    n_pre = -(-(window - 1) // qblock) * qblock
    span = n_pre + qblock
    kp = jnp.pad(k, ((0, 0), (0, 0), (n_pre, 0), (0, 0)))
    vp = jnp.pad(v, ((0, 0), (0, 0), (n_pre, 0), (0, 0)))
    nqb = s // qblock
    qb = q.reshape(b, hk, g, nqb, qblock, dh).transpose(3, 0, 1, 2, 4, 5)

    def one_block(args):
        i, qi = args
        start = i * qblock
        ks = lax.dynamic_slice_in_dim(kp, start, span, axis=2)
        vs = lax.dynamic_slice_in_dim(vp, start, span, axis=2)
        t = start + jnp.arange(qblock)
        p = start - n_pre + jnp.arange(span)
        dist = t[:, None] - p[None, :]
        mask = (dist >= 0) & (dist < window) & (p[None, :] >= 0)
        sc = jnp.einsum('bhgqd,bhkd->bhgqk', qi, ks, preferred_element_type=jnp.float32) * scale
        pr = masked_softmax(sc, mask, sink)
        return jnp.einsum('bhgqk,bhkd->bhgqd', pr.astype(v.dtype), vs)

    out = lax.map(one_block, (jnp.arange(nqb), qb))
    return out.transpose(1, 2, 3, 0, 4, 5).reshape(b, hk, g, s, dh)


def moba_mixer(q, k, v, cos, sin):
    b, s, _ = q.shape
    hq, dh, blk = MOBA_HEADS, HEAD_DIM, MOBA_BLOCK
    scale = dh ** -0.5
    qh = apply_rope(to_heads(q, hq), cos, sin)
    kh = apply_rope(to_heads(k, hq), cos, sin)
    vh = to_heads(v, hq)
    nb = -(-s // blk)
    pad = nb * blk - s
    kb = jnp.pad(kh, ((0, 0), (0, 0), (0, pad), (0, 0))).reshape(b, hq, nb, blk, dh)
    vb = jnp.pad(vh, ((0, 0), (0, 0), (0, pad), (0, 0))).reshape(b, hq, nb, blk, dh)
    t = jnp.arange(s)
    gate = jnp.einsum('bhsd,bhnd->bhsn', qh, jnp.mean(kb, axis=3), preferred_element_type=jnp.float32)
    gate = jnp.where(jnp.arange(nb)[None, :] < (t // blk)[:, None], gate, -jnp.inf)
    n_sel = min(MOBA_TOPK, nb)
    _, sel = lax.top_k(gate, n_sel)
    qc_n = MOBA_QCHUNK
    nqc = s // qc_n
    q_ch = qh.reshape(b, hq, nqc, qc_n, dh).transpose(2, 0, 1, 3, 4)
    sel_ch = sel.reshape(b, hq, nqc, qc_n, n_sel).transpose(2, 0, 1, 3, 4)
    bi = jnp.arange(b)[:, None, None, None]
    hi = jnp.arange(hq)[None, :, None, None]

    def chunk(args):
        ci, qc, ic = args
        tc = ci * qc_n + jnp.arange(qc_n)
        own = (ci * qc_n) // blk
        kg = kb[bi, hi, ic]
        vg = vb[bi, hi, ic]
        ko = lax.dynamic_index_in_dim(kb, own, axis=2, keepdims=False)
        vo = lax.dynamic_index_in_dim(vb, own, axis=2, keepdims=False)
        s_sel = jnp.einsum('bhqd,bhqnkd->bhqnk', qc, kg, preferred_element_type=jnp.float32)
        s_own = jnp.einsum('bhqd,bhkd->bhqk', qc, ko, preferred_element_type=jnp.float32)
        sc = jnp.concatenate([s_sel.reshape(b, hq, qc_n, n_sel * blk), s_own], axis=-1) * scale
        sel_ok = jnp.repeat(jnp.arange(n_sel) < own, blk)
        own_ok = (own * blk + jnp.arange(blk))[None, :] <= tc[:, None]
        mask = jnp.concatenate([jnp.broadcast_to(sel_ok[None, :], (qc_n, n_sel * blk)), own_ok], axis=-1)
        pr = masked_softmax(sc, mask).astype(v.dtype)
        o = jnp.einsum('bhqnk,bhqnkd->bhqd', pr[..., :n_sel * blk].reshape(b, hq, qc_n, n_sel, blk), vg)
        return o + jnp.einsum('bhqk,bhkd->bhqd', pr[..., n_sel * blk:], vo)

    out = lax.map(chunk, (jnp.arange(nqc), q_ch, sel_ch))
    return from_heads(out.transpose(1, 2, 0, 3, 4).reshape(b, hq, s, dh))


def swa_mixer(q, k, v, sink, cos, sin):
    grp = SWA_HEADS // SWA_KV_HEADS
    qh = apply_rope(to_gqa(q, SWA_KV_HEADS, grp), cos, sin)
    kh = apply_rope(to_heads(k, SWA_KV_HEADS), cos, sin)
    vh = to_heads(v, SWA_KV_HEADS)
    sink_logit = sink.astype(jnp.float32).reshape(1, SWA_KV_HEADS, grp, 1, 1)
    return from_gqa(banded_attention(qh, kh, vh, SWA_WINDOW, SWA_QBLOCK, sink_logit))


def retention_mixer(q, k, v, cos, sin):
    b, s, _ = q.shape
    f32 = jnp.float32
    c = RET_CHUNK
    nc = s // c
    qh = apply_rope(to_heads(q, RET_HEADS), cos, sin).astype(f32)
    kh = apply_rope(to_heads(k, RET_HEADS), cos, sin).astype(f32) * (RET_QK_DIM ** -0.5)
    vh = to_heads(v, RET_HEADS).astype(f32)
    log_g = jnp.log(1.0 - 2.0 ** (-5.0 - jnp.arange(RET_HEADS, dtype=f32)))
    idx = jnp.arange(c, dtype=f32)
    diff = idx[:, None] - idx[None, :]
    intra = jnp.where(diff >= 0, jnp.exp(jnp.maximum(diff, 0.0) * log_g[:, None, None]), 0.0)
    q_dec = jnp.exp((idx + 1.0) * log_g[:, None])[None, :, :, None]
    k_dec = jnp.exp((c - 1.0 - idx) * log_g[:, None])[None, :, :, None]
    chunk_dec = jnp.exp(c * log_g)[None, :, None, None]

    def chunks(t):
        return t.reshape(b, RET_HEADS, nc, c, -1).transpose(2, 0, 1, 3, 4)

    def step(state, xs):
        qc, kc, vc = xs
        att = jnp.einsum('bhqd,bhkd->bhqk', qc, kc) * intra
        o = jnp.einsum('bhqk,bhke->bhqe', att, vc) + jnp.einsum('bhqd,bhde->bhqe', qc, state) * q_dec
        state = state * chunk_dec + jnp.einsum('bhkd,bhke->bhde', kc * k_dec, vc)
        return state, o

    state0 = jnp.zeros((b, RET_HEADS, RET_QK_DIM, RET_V_DIM), f32)
    _, o = lax.scan(step, state0, (chunks(qh), chunks(kh), chunks(vh)))
    o = o.transpose(1, 2, 0, 3, 4).reshape(b, RET_HEADS, s, RET_V_DIM)
    mu = jnp.mean(o, axis=-1, keepdims=True)
    var = jnp.mean(jnp.square(o - mu), axis=-1, keepdims=True)
    o = (o - mu) * lax.rsqrt(var + NORM_EPS)
    return from_heads(o).astype(q.dtype)


def nsa_overlap_matrix(n_cmp, n_slc):
    c0 = np.arange(n_cmp)[:, None] * NSA_CMP_STRIDE
    s0 = np.arange(n_slc)[None, :] * NSA_SLC_BLOCK
    ov = np.minimum(c0 + NSA_CMP_LEN, s0 + NSA_SLC_BLOCK) - np.maximum(c0, s0)
    return jnp.asarray(np.clip(ov, 0, None) / NSA_CMP_LEN, dtype=jnp.float32)


def nsa_mixer(q, kc, vc, ks, vs, kw, vw, g, cmp_pos, w_ck1, w_ck2, w_cv1, w_cv2, cos, sin):
    b, s, _ = q.shape
    hk = NSA_KV_HEADS
    grp = NSA_HEADS // NSA_KV_HEADS
    dh = HEAD_DIM
    scale = dh ** -0.5
    t = jnp.arange(s)
    qh = apply_rope(to_gqa(q, hk, grp), cos, sin)

    n_cmp = (s - NSA_CMP_LEN) // NSA_CMP_STRIDE + 1
    blk_idx = jnp.arange(n_cmp)[:, None] * NSA_CMP_STRIDE + jnp.arange(NSA_CMP_LEN)[None, :]
    cmp_end = jnp.arange(n_cmp) * NSA_CMP_STRIDE + NSA_CMP_LEN - 1

    def compress(tok, w1, w2):
        blocks = to_heads(tok, hk)[:, :, blk_idx] + cmp_pos
        flat = blocks.reshape(b, hk, n_cmp, NSA_CMP_LEN * dh)
        return jax.nn.gelu(flat @ w1) @ w2

    k_cmp = apply_rope(compress(kc, w_ck1, w_ck2), cos[cmp_end], sin[cmp_end])
    v_cmp = compress(vc, w_cv1, w_cv2)
    s_c = jnp.einsum('bhgqd,bhnd->bhgqn', qh, k_cmp, preferred_element_type=jnp.float32) * scale
    p_c = masked_softmax(s_c, cmp_end[None, :] <= t[:, None])
    o_cmp = jnp.einsum('bhgqn,bhnd->bhgqd', p_c.astype(v_cmp.dtype), v_cmp)

    n_slc = s // NSA_SLC_BLOCK
    top_n = min(NSA_SLC_TOPN, n_slc)
    imp = jnp.einsum('bhgqn,nj->bhqj', p_c, nsa_overlap_matrix(n_cmp, n_slc))
    q_blk = t // NSA_SLC_BLOCK
    j = jnp.arange(n_slc)[None, :]
    forced = (j == 0) | (j == q_blk[:, None]) | (j == q_blk[:, None] - 1)
    imp = jnp.where(forced, jnp.inf, imp)
    imp = jnp.where(j <= q_blk[:, None], imp, -jnp.inf)
    _, sel = lax.top_k(imp, top_n)
    ksb = apply_rope(to_heads(ks, hk), cos, sin).reshape(b, hk, n_slc, NSA_SLC_BLOCK, dh)
    vsb = to_heads(vs, hk).reshape(b, hk, n_slc, NSA_SLC_BLOCK, dh)
    qc_n = NSA_QCHUNK
    nqc = s // qc_n
    q_ch = qh.reshape(b, hk, grp, nqc, qc_n, dh).transpose(3, 0, 1, 2, 4, 5)
    sel_ch = sel.reshape(b, hk, nqc, qc_n, top_n).transpose(2, 0, 1, 3, 4)
    bi = jnp.arange(b)[:, None, None, None]
    hi = jnp.arange(hk)[None, :, None, None]
    kpos = jnp.arange(NSA_SLC_BLOCK)
    n_keys = top_n * NSA_SLC_BLOCK

    def sel_chunk(args):
        ci, qc, ic = args
        tc = ci * qc_n + jnp.arange(qc_n)
        kg = ksb[bi, hi, ic]
        vg = vsb[bi, hi, ic]
        sc = jnp.einsum('bhgqd,bhqnkd->bhgqnk', qc, kg, preferred_element_type=jnp.float32) * scale
        key_pos = ic[..., None] * NSA_SLC_BLOCK + kpos
        rank_ok = jnp.arange(top_n)[None, :] <= (tc // NSA_SLC_BLOCK)[:, None]
        ok = (key_pos <= tc[None, None, :, None, None]) & rank_ok[None, None, :, :, None]
        pr = masked_softmax(sc.reshape(b, hk, grp, qc_n, n_keys), ok.reshape(b, hk, 1, qc_n, n_keys))
        pr = pr.reshape(b, hk, grp, qc_n, top_n, NSA_SLC_BLOCK).astype(vg.dtype)
        return jnp.einsum('bhgqnk,bhqnkd->bhgqd', pr, vg)

    o_slc = lax.map(sel_chunk, (jnp.arange(nqc), q_ch, sel_ch))
    o_slc = o_slc.transpose(1, 2, 3, 0, 4, 5).reshape(b, hk, grp, s, dh)

    kwh = apply_rope(to_heads(kw, hk), cos, sin)
    o_win = banded_attention(qh, kwh, to_heads(vw, hk), NSA_WINDOW, NSA_WIN_QBLOCK)

    gates = jax.nn.sigmoid(g).reshape(b, s, 3, hk, grp).transpose(2, 0, 3, 4, 1)[..., None]
    o = gates[0] * o_cmp + gates[1] * o_slc + gates[2] * o_win
    return from_gqa(o)


def hybrid_layer(x, cos, sin, norm_g, w_in, w_branch, w_out, swa_sink, cmp_pos, w_ck1, w_ck2, w_cv1, w_cv2):
    b, s, d = x.shape
    h = rms_norm(x, norm_g)
    c = split_cols(h @ w_in)
    silu = jax.nn.silu
    outs = (
        moba_mixer(c['moba_q'], c['moba_k'], c['moba_v'], cos, sin) * silu(c['moba_z']),
        swa_mixer(c['swa_q'], c['swa_k'], c['swa_v'], swa_sink, cos, sin) * silu(c['swa_z']),
        retention_mixer(c['ret_q'], c['ret_k'], c['ret_v'], cos, sin) * silu(c['ret_z']),
        nsa_mixer(c['nsa_q'], c['nsa_kc'], c['nsa_vc'], c['nsa_ks'], c['nsa_vs'], c['nsa_kw'], c['nsa_vw'],
                  c['nsa_g'], cmp_pos, w_ck1, w_ck2, w_cv1, w_cv2, cos, sin) * silu(c['nsa_z']),
    )
    gates = jax.nn.sigmoid(c['merge_g']).reshape(b, s, N_BRANCH, d)
    merged = jnp.zeros_like(x)
    row = 0
    for i in range(N_BRANCH):
        width = BRANCH_WIDTHS[i]
        merged = merged + gates[:, :, i] * (outs[i] @ w_branch[row:row + width])
        row += width
    return x + merged @ w_out


def setup_inputs(seed: int = 0) -> dict:
    key = jax.random.key(seed)
    ks = jax.random.split(key, 12)
    f32 = jnp.float32
    l_dh = NSA_CMP_LEN * HEAD_DIM

    def nrm(k, shape, scale):
        return scale * jax.random.normal(k, shape, f32)

    return {
        'x': nrm(ks[0], (BATCH, SEQ, D_MODEL), 1.0),
        'norm_g': 1.0 + nrm(ks[1], (DEPTH, D_MODEL), 0.02),
        'w_in': nrm(ks[2], (DEPTH, D_MODEL, D_IN), D_MODEL ** -0.5),
        'w_branch': nrm(ks[3], (DEPTH, D_BRANCH, D_MODEL), MOBA_W ** -0.5),
        'w_out': nrm(ks[4], (DEPTH, D_MODEL, D_MODEL), D_MODEL ** -0.5),
        'swa_sink': nrm(ks[5], (DEPTH, SWA_HEADS), 0.5),
        'nsa_cmp_pos': nrm(ks[6], (DEPTH, NSA_CMP_LEN, HEAD_DIM), 0.1),
        'nsa_w_ck1': nrm(ks[7], (DEPTH, l_dh, NSA_CMP_HIDDEN), l_dh ** -0.5),
        'nsa_w_ck2': nrm(ks[8], (DEPTH, NSA_CMP_HIDDEN, HEAD_DIM), NSA_CMP_HIDDEN ** -0.5),
        'nsa_w_cv1': nrm(ks[9], (DEPTH, l_dh, NSA_CMP_HIDDEN), l_dh ** -0.5),
        'nsa_w_cv2': nrm(ks[10], (DEPTH, NSA_CMP_HIDDEN, HEAD_DIM), NSA_CMP_HIDDEN ** -0.5),
        'final_norm_g': 1.0 + nrm(ks[11], (D_MODEL,), 0.02),
    }


def reference(x, norm_g, w_in, w_branch, w_out, swa_sink, nsa_cmp_pos, nsa_w_ck1, nsa_w_ck2,
              nsa_w_cv1, nsa_w_cv2, final_norm_g):
    cos, sin = rope_tables(x.shape[1])
    for l in range(DEPTH):
        x = hybrid_layer(x, cos, sin, norm_g[l], w_in[l], w_branch[l], w_out[l], swa_sink[l],
                         nsa_cmp_pos[l], nsa_w_ck1[l], nsa_w_ck2[l], nsa_w_cv1[l], nsa_w_cv2[l])
    return rms_norm(x, final_norm_g)
```

```python
import functools
import math

import numpy as np
import jax
import jax.numpy as jnp
from jax import lax
from jax.experimental import pallas as pl
from jax.experimental.pallas import tpu as pltpu

F32 = jnp.float32
BF16 = jnp.bfloat16

D_MODEL = 1024
HEAD_DIM = 64
ROPE_THETA = 10000.0
NORM_EPS = 1e-6
TINY = 1e-30
N_BRANCH = 4
NEG = -1e30

MOBA_HEADS = 8
MOBA_BLOCK = 256
MOBA_TOPK = 3

SWA_HEADS = 8
SWA_KV_HEADS = 2
SWA_WINDOW = 128

RET_HEADS = 4
RET_QK_DIM = 64
RET_V_DIM = 128

NSA_HEADS = 8
NSA_KV_HEADS = 2
NSA_CMP_LEN = 32
NSA_CMP_STRIDE = 16
NSA_CMP_HIDDEN = 256
NSA_SLC_BLOCK = 64
NSA_SLC_TOPN = 16
NSA_WINDOW = 512

LANES = 128
TQ = 256
PROJ_TM = 1024
PROJ_TN = 512
MERGE_TM = 512
RET_CHUNK = 256

_IN_SPLITS = (
    ('moba_q', 512), ('moba_k', 512), ('moba_v', 512), ('moba_z', 512),
    ('swa_q', 512), ('swa_k', 128), ('swa_v', 128), ('swa_z', 512),
    ('ret_q', 256), ('ret_k', 256), ('ret_v', 512), ('ret_z', 512),
    ('nsa_q', 512), ('nsa_kc', 128), ('nsa_vc', 128),
    ('nsa_ks', 128), ('nsa_vs', 128), ('nsa_kw', 128), ('nsa_vw', 128),
    ('nsa_g', 24), ('nsa_z', 512),
    ('merge_g', 4096),
)
_WIDTH = dict(_IN_SPLITS)
_SLAB = (
    ('merge_g', 4096),
    ('moba_q', 512), ('moba_k', 512), ('swa_q', 512), ('nsa_q', 512),
    ('ret_q', 256), ('ret_k', 256),
    ('swa_k', 128), ('nsa_ks', 128), ('nsa_kw', 128), (None, 128),
    ('moba_v', 512), ('moba_z', 512), ('swa_z', 512), ('ret_v', 512), ('ret_z', 512), ('nsa_z', 512),
    ('swa_v', 128), ('nsa_vs', 128), ('nsa_vw', 128), ('nsa_g', 128),
    ('nsa_kc', 128), ('nsa_vc', 128), (None, 256),
)
ROPE_TILE_LO, ROPE_TILE_HI = 8, 14
_COL_SCALE = {'moba_q': 0.125, 'swa_q': 0.125, 'nsa_q': 0.125, 'ret_k': 0.125}


def _slab_layout():
    src_off, off = {}, 0
    for name, w in _IN_SPLITS:
        src_off[name] = off
        off += w
    idx, scale, col = [], [], {}
    pos = 0
    for name, w in _SLAB:
        if name is not None:
            col[name] = pos
            real = _WIDTH[name]
            idx += list(range(src_off[name], src_off[name] + real)) + [-1] * (w - real)
            scale += [_COL_SCALE.get(name, 1.0)] * real + [0.0] * (w - real)
        else:
            idx += [-1] * w
            scale += [0.0] * w
        pos += w
    return np.asarray(idx, np.int32), np.asarray(scale, np.float32), col, pos


_SLAB_IDX, _SLAB_SCALE, COL, D_SLAB = _slab_layout()


def _cb(name, width=LANES):
    assert COL[name] % width == 0
    return COL[name] // width


NT = (((1,), (1,)), ((), ()))


def _dot(a, b, precision=None):
    return jnp.dot(a, b, preferred_element_type=F32, precision=precision)


def _dot_nt(a, b, precision=None):
    return lax.dot_general(a, b, NT, preferred_element_type=F32, precision=precision)


def _silu(z):
    return z * jax.nn.sigmoid(z)


def _rope_chunk(y, cos, sin_a, sin_b):
    return y * cos + pltpu.roll(y, 96, 1) * sin_a + pltpu.roll(y, 32, 1) * sin_b


def _proj_kernel(x_ref, g_ref, w_ref, cos_ref, sa_ref, sb_ref, o_ref, h_ref):
    j = pl.program_id(1)

    @pl.when(j == 0)
    def _():
        x = x_ref[...]
        ms = jnp.mean(x * x, axis=-1, keepdims=True)
        h_ref[...] = (x * lax.rsqrt(ms + NORM_EPS) * g_ref[...]).astype(BF16)

    y = _dot(h_ref[...], w_ref[...])
    is_rope = jnp.logical_and(j >= ROPE_TILE_LO, j < ROPE_TILE_HI)

    @pl.when(is_rope)
    def _():
        cos, sa, sb = cos_ref[...], sa_ref[...], sb_ref[...]
        for c in range(PROJ_TN // LANES):
            sl = slice(c * LANES, (c + 1) * LANES)
            o_ref[:, sl] = _rope_chunk(y[:, sl], cos, sa, sb).astype(BF16)

    @pl.when(jnp.logical_not(is_rope))
    def _():
        o_ref[...] = y.astype(BF16)


def _proj_in(x2, norm_g, w_slab, cos_t, sa_t, sb_t, seq):
    t = x2.shape[0]
    tm = min(PROJ_TM, seq)
    per_seq = seq // tm
    return pl.pallas_call(
        _proj_kernel,
        grid=(t // tm, D_SLAB // PROJ_TN),
        in_specs=[
            pl.BlockSpec((tm, D_MODEL), lambda i, j: (i, 0)),
            pl.BlockSpec((1, D_MODEL), lambda i, j: (0, 0)),
            pl.BlockSpec((D_MODEL, PROJ_TN), lambda i, j: (0, j)),
            pl.BlockSpec((tm, LANES), lambda i, j: (i % per_seq, 0)),
            pl.BlockSpec((tm, LANES), lambda i, j: (i % per_seq, 0)),
            pl.BlockSpec((tm, LANES), lambda i, j: (i % per_seq, 0)),
        ],
        out_specs=pl.BlockSpec((tm, PROJ_TN), lambda i, j: (i, j)),
        out_shape=jax.ShapeDtypeStruct((t, D_SLAB), BF16),
        scratch_shapes=[pltpu.VMEM((tm, D_MODEL), BF16)],
        compiler_params=pltpu.CompilerParams(dimension_semantics=("parallel", "arbitrary")),
        name="proj_in",
    )(x2, norm_g.reshape(1, D_MODEL), w_slab, cos_t, sa_t, sb_t)


def _rank_before(scores, n):
    j_iota = lax.broadcasted_iota(jnp.int32, scores.shape, 0)
    cnt = jnp.zeros(scores.shape, F32)
    for jp in range(n):
        r = scores[jp:jp + 1, :]
        ahead = jnp.logical_or(r > scores, jnp.logical_and(r == scores, jp < j_iota))
        cnt = cnt + jnp.where(ahead, 1.0, 0.0)
    return cnt


def _flash_causal(qaug, k_at, v_at, t, n_lanes_out):
    row = lax.broadcasted_iota(jnp.int32, (TQ, TQ), 0)
    col = lax.broadcasted_iota(jnp.int32, (TQ, TQ), 1)

    def body(j, carry):
        m, l, acc = carry
        s = _dot_nt(qaug, k_at(j))
        s = jnp.where(col + j * TQ <= row + t * TQ, s, NEG)
        m_new = jnp.maximum(m, jnp.max(s, axis=-1, keepdims=True))
        alpha = jnp.exp(m - m_new)
        p = jnp.exp(s - m_new)
        l = alpha * l + jnp.sum(p, axis=-1, keepdims=True)
        acc = alpha * acc + _dot(p.astype(BF16), v_at(j))
        return m_new, l, acc

    m0 = jnp.full((TQ, 1), NEG, F32)
    l0 = jnp.zeros((TQ, 1), F32)
    a0 = jnp.zeros((TQ, n_lanes_out), F32)
    m, l, acc = lax.fori_loop(0, t + 1, body, (m0, l0, a0))
    return acc / l


def _bias_lanes(allowed_t, row0):
    n = allowed_t.shape[0]
    pieces = []
    if row0:
        pieces.append(jnp.zeros((row0, TQ), F32))
    pieces.append(allowed_t)
    if LANES - row0 - n:
        pieces.append(jnp.zeros((LANES - row0 - n, TQ), F32))
    full = jnp.concatenate(pieces, axis=0)
    return ((full.T - 1.0) * (-NEG)).astype(BF16)


def _moba_kernel(q_ref, k_ref, v_ref, z_ref, o_ref, kaug_ref, kmean_ref, *, seq):
    t = pl.program_id(2)
    nb = seq // MOBA_BLOCK

    @pl.when(t == 0)
    def _():
        k = k_ref[...]
        kf = k.astype(F32)
        rowblk = lax.broadcasted_iota(jnp.int32, (seq, LANES), 0) // MOBA_BLOCK
        lane = lax.broadcasted_iota(jnp.int32, (seq, LANES), 1)
        kaug_ref[0] = jnp.where(lane < HEAD_DIM, kf, jnp.where(lane - HEAD_DIM == rowblk, 1.0, 0.0)).astype(BF16)
        kaug_ref[1] = jnp.where(lane >= HEAD_DIM, kf, jnp.where(lane == rowblk, 1.0, 0.0)).astype(BF16)
        blk = lax.broadcasted_iota(jnp.int32, (nb, seq), 0)
        pos = lax.broadcasted_iota(jnp.int32, (nb, seq), 1) // MOBA_BLOCK
        avg = jnp.where(blk == pos, 1.0 / MOBA_BLOCK, 0.0).astype(BF16)
        kmean_ref[...] = _dot(avg, k)

    q = q_ref[...]
    lane = lax.broadcasted_iota(jnp.int32, (TQ, LANES), 1)
    j_iota = lax.broadcasted_iota(jnp.int32, (nb, TQ), 0)
    outs = []
    for h in range(2):
        mine = (lane < HEAD_DIM) if h == 0 else (lane >= HEAD_DIM)
        qm = jnp.where(mine, q, jnp.zeros((), BF16))
        gate = _dot_nt(kmean_ref[...], qm.astype(F32), precision=lax.Precision.HIGHEST)
        gate = jnp.where(j_iota < t, gate, -jnp.inf)
        cnt = _rank_before(gate, nb)
        allowed = jnp.logical_or(jnp.logical_and(cnt < MOBA_TOPK, j_iota < t), j_iota == t)
        bias = _bias_lanes(jnp.where(allowed, 1.0, 0.0), HEAD_DIM if h == 0 else 0)
        qaug = jnp.where(mine, q, bias)
        o = _flash_causal(
            qaug,
            lambda j, h=h: kaug_ref[h, pl.ds(pl.multiple_of(j * TQ, TQ), TQ), :],
            lambda j: v_ref[pl.ds(pl.multiple_of(j * TQ, TQ), TQ), :],
            t, LANES)
        outs.append(o)
    o = jnp.where(lane < HEAD_DIM, outs[0], outs[1])
    o_ref[...] = (o * _silu(z_ref[...].astype(F32))).astype(BF16)


def _moba(slab, batch, seq):
    nt = seq // TQ
    return pl.pallas_call(
        functools.partial(_moba_kernel, seq=seq),
        grid=(batch, MOBA_HEADS // 2, nt),
        in_specs=[
            pl.BlockSpec((TQ, LANES), lambda b, p, t: (b * nt + t, _cb('moba_q') + p)),
            pl.BlockSpec((seq, LANES), lambda b, p, t: (b, _cb('moba_k') + p)),
            pl.BlockSpec((seq, LANES), lambda b, p, t: (b, _cb('moba_v') + p)),
            pl.BlockSpec((TQ, LANES), lambda b, p, t: (b * nt + t, _cb('moba_z') + p)),
        ],
        out_specs=pl.BlockSpec((TQ, LANES), lambda b, p, t: (b * nt + t, p)),
        out_shape=jax.ShapeDtypeStruct((batch * seq, MOBA_HEADS * HEAD_DIM), BF16),
        scratch_shapes=[pltpu.VMEM((2, seq, LANES), BF16), pltpu.VMEM((seq // MOBA_BLOCK, LANES), F32)],
        compiler_params=pltpu.CompilerParams(dimension_semantics=("parallel", "parallel", "arbitrary")),
        name="moba",
    )(slab, slab, slab, slab)


def _rep2(x, hk):
    lane = lax.broadcasted_iota(jnp.int32, x.shape, 1)
    keep = (lane < HEAD_DIM) == (hk == 0)
    return jnp.where(keep, x, pltpu.roll(x, HEAD_DIM, 1))


def _head_mask(g, width):
    lane = lax.broadcasted_iota(jnp.int32, (TQ, width), 1)
    return (lane // HEAD_DIM) == g


SWA_SPAN = TQ + LANES


def _swa_kernel(sink_ref, q_ref, k_ref, v_ref, z_ref, o_ref, krep_ref, vrep_ref):
    hk = pl.program_id(1)
    t = pl.program_id(2)

    @pl.when(t == 0)
    def _():
        kr = _rep2(k_ref[...].astype(F32), hk).astype(BF16)
        vr = _rep2(v_ref[...].astype(F32), hk).astype(BF16)
        krep_ref[...] = jnp.concatenate([kr, kr], axis=1)
        vrep_ref[...] = jnp.concatenate([vr, vr], axis=1)

    q = q_ref[...]
    k0 = pl.multiple_of(jnp.maximum(t * TQ - LANES, 0), LANES)
    kw = krep_ref[pl.ds(k0, SWA_SPAN), :]
    vw = vrep_ref[pl.ds(k0, SWA_SPAN), :]
    dist = (t * TQ + lax.broadcasted_iota(jnp.int32, (TQ, SWA_SPAN), 0)) - (
        k0 + lax.broadcasted_iota(jnp.int32, (TQ, SWA_SPAN), 1))
    mask = jnp.logical_and(dist >= 0, dist < SWA_WINDOW)
    out = jnp.zeros((TQ, 4 * HEAD_DIM), F32)
    for g in range(4):
        mine = _head_mask(g, 4 * HEAD_DIM)
        s = _dot_nt(jnp.where(mine, q, jnp.zeros((), BF16)), kw)
        s = jnp.where(mask, s, NEG)
        sink = sink_ref[hk * 4 + g]
        m = jnp.maximum(jnp.max(s, axis=-1, keepdims=True), sink)
        e = jnp.exp(s - m)
        den = jnp.sum(e, axis=-1, keepdims=True) + jnp.exp(sink - m)
        p = e / jnp.maximum(den, TINY)
        out = jnp.where(mine, _dot(p.astype(BF16), vw), out)
    o_ref[...] = (out * _silu(z_ref[...].astype(F32))).astype(BF16)


def _swa(slab, sink, batch, seq):
    nt = seq // TQ
    w = 4 * HEAD_DIM
    return pl.pallas_call(
        _swa_kernel,
        grid=(batch, SWA_KV_HEADS, nt),
        in_specs=[
            pl.BlockSpec(memory_space=pltpu.SMEM),
            pl.BlockSpec((TQ, w), lambda b, h, t: (b * nt + t, _cb('swa_q', w) + h)),
            pl.BlockSpec((seq, LANES), lambda b, h, t: (b, _cb('swa_k'))),
            pl.BlockSpec((seq, LANES), lambda b, h, t: (b, _cb('swa_v'))),
            pl.BlockSpec((TQ, w), lambda b, h, t: (b * nt + t, _cb('swa_z', w) + h)),
        ],
        out_specs=pl.BlockSpec((TQ, w), lambda b, h, t: (b * nt + t, h)),
        out_shape=jax.ShapeDtypeStruct((batch * seq, SWA_HEADS * HEAD_DIM), BF16),
        scratch_shapes=[pltpu.VMEM((seq, w), BF16), pltpu.VMEM((seq, w), BF16)],
        compiler_params=pltpu.CompilerParams(dimension_semantics=("parallel", "parallel", "arbitrary")),
        name="swa",
    )(sink, slab, slab, slab, slab)


def _ret_kernel(qk_ref, v_ref, z_ref, o_ref, state_ref):
    c = RET_CHUNK

    @pl.when(pl.program_id(1) == 0)
    def _():
        state_ref[...] = jnp.zeros(state_ref.shape, F32)

    q = qk_ref[:, :RET_HEADS * RET_QK_DIM]
    k = qk_ref[:, RET_HEADS * RET_QK_DIM:]
    ii = lax.broadcasted_iota(jnp.int32, (c, c), 0)
    jj = lax.broadcasted_iota(jnp.int32, (c, c), 1)
    diff = (ii - jj).astype(F32)
    row = lax.broadcasted_iota(jnp.int32, (c, 1), 0).astype(F32)
    lane = lax.broadcasted_iota(jnp.int32, (c, RET_HEADS * RET_QK_DIM), 1)
    for h in range(RET_HEADS):
        log_g = math.log(1.0 - 2.0 ** (-5.0 - h))
        intra = jnp.where(diff >= 0, jnp.exp(jnp.maximum(diff, 0.0) * log_g), 0.0)
        q_dec = jnp.exp((row + 1.0) * log_g)
        k_dec = jnp.exp((c - 1.0 - row) * log_g)
        chunk_dec = math.exp(c * log_g)
        qm = jnp.where((lane // RET_QK_DIM) == h, q, jnp.zeros((), BF16))
        vh = v_ref[:, h * RET_V_DIM:(h + 1) * RET_V_DIM]
        att = _dot_nt(qm, k) * intra
        st = state_ref[h]
        o = _dot(att.astype(BF16), vh) + _dot(qm, st.astype(BF16)) * q_dec
        kd = (k.astype(F32) * k_dec).T.astype(BF16)
        state_ref[h] = st * chunk_dec + _dot(kd, vh)
        mu = jnp.mean(o, axis=-1, keepdims=True)
        var = jnp.mean(jnp.square(o - mu), axis=-1, keepdims=True)
        o = (o - mu) * lax.rsqrt(var + NORM_EPS)
        zh = z_ref[:, h * RET_V_DIM:(h + 1) * RET_V_DIM].astype(F32)
        o_ref[:, h * RET_V_DIM:(h + 1) * RET_V_DIM] = (o * _silu(zh)).astype(BF16)


def _retention(slab, batch, seq):
    nc = seq // RET_CHUNK
    w = RET_HEADS * RET_V_DIM
    return pl.pallas_call(
        _ret_kernel,
        grid=(batch, nc),
        in_specs=[
            pl.BlockSpec((RET_CHUNK, w), lambda b, c: (b * nc + c, _cb('ret_q', w))),
            pl.BlockSpec((RET_CHUNK, w), lambda b, c: (b * nc + c, _cb('ret_v', w))),
            pl.BlockSpec((RET_CHUNK, w), lambda b, c: (b * nc + c, _cb('ret_z', w))),
        ],
        out_specs=pl.BlockSpec((RET_CHUNK, w), lambda b, c: (b * nc + c, 0)),
        out_shape=jax.ShapeDtypeStruct((batch * seq, w), BF16),
        scratch_shapes=[pltpu.VMEM((RET_HEADS, RET_HEADS * RET_QK_DIM, RET_V_DIM), F32)],
        compiler_params=pltpu.CompilerParams(dimension_semantics=("parallel", "arbitrary")),
        name="retention",
    )(slab, slab, slab)


def _compress_kernel(r_ref, w1_ref, w2_ref, pos_ref, cos_ref, sa_ref, sb_ref, o_ref):
    kind = pl.program_id(0) // NSA_KV_HEADS
    half = NSA_CMP_STRIDE * HEAD_DIM
    r = r_ref[...]
    n_rows = r.shape[0]
    lo = _dot(r, w1_ref[:half, :])
    hi = _dot(r, w1_ref[half:, :])
    pos = _dot(pos_ref[...], w1_ref[...])[0:1, :]
    pre = lo + pltpu.roll(hi, n_rows - 1, 0) + pos
    y = _dot(jax.nn.gelu(pre).astype(BF16), w2_ref[...])
    roped = _rope_chunk(y, cos_ref[...], sa_ref[...], sb_ref[...])
    o_ref[...] = jnp.where(kind == 0, roped, y)


def _compress(r, w1, w2, pos8, cos_c, sa_c, sb_c, batch):
    n_rows = r.shape[2]
    flat = NSA_CMP_LEN * HEAD_DIM
    return pl.pallas_call(
        _compress_kernel,
        grid=(2 * NSA_KV_HEADS, batch),
        in_specs=[
            pl.BlockSpec((None, None, n_rows, flat // 2), lambda i, b: (i, b, 0, 0)),
            pl.BlockSpec((None, flat, NSA_CMP_HIDDEN), lambda i, b: (i // NSA_KV_HEADS, 0, 0)),
            pl.BlockSpec((None, NSA_CMP_HIDDEN, LANES), lambda i, b: (i // NSA_KV_HEADS, 0, 0)),
            pl.BlockSpec((8, flat), lambda i, b: (0, 0)),
            pl.BlockSpec((n_rows, LANES), lambda i, b: (0, 0)),
            pl.BlockSpec((n_rows, LANES), lambda i, b: (0, 0)),
            pl.BlockSpec((n_rows, LANES), lambda i, b: (0, 0)),
        ],
        out_specs=pl.BlockSpec((None, None, n_rows, LANES), lambda i, b: (i, b, 0, 0)),
        out_shape=jax.ShapeDtypeStruct((2 * NSA_KV_HEADS, batch, n_rows, LANES), F32),
        compiler_params=pltpu.CompilerParams(dimension_semantics=("parallel", "parallel")),
        name="nsa_compress",
    )(r, w1, w2, pos8, cos_c, sa_c, sb_c)


NSA_WIN_SPAN = TQ + NSA_WINDOW
SEL_LANE0 = 64


def _nsa_kernel(q_ref, ks_ref, vs_ref, kw_ref, vw_ref, kc_ref, vc_ref, g_ref, z_ref, o_ref,
                ksa_ref, ksb_ref, vsr_ref, kwr_ref, vwr_ref, kcr_ref, vcr_ref, *, seq):
    hk = pl.program_id(1)
    t = pl.program_id(2)
    n_slc = seq // NSA_SLC_BLOCK
    n_cmp = (seq - NSA_CMP_LEN) // NSA_CMP_STRIDE + 1
    n_crow = kc_ref.shape[0]
    w = 4 * HEAD_DIM

    @pl.when(t == 0)
    def _():
        ks = _rep2(ks_ref[...].astype(F32), hk).astype(BF16)
        rowblk = lax.broadcasted_iota(jnp.int32, (seq, LANES), 0) // NSA_SLC_BLOCK
        lane = lax.broadcasted_iota(jnp.int32, (seq, LANES), 1)
        onehot = jnp.where(lane - SEL_LANE0 == rowblk, 1.0, 0.0).astype(BF16)
        ksa_ref[...] = jnp.concatenate([ks, onehot], axis=1)
        ksb_ref[...] = jnp.concatenate([onehot, ks], axis=1)
        vs = _rep2(vs_ref[...].astype(F32), hk).astype(BF16)
        vsr_ref[...] = jnp.concatenate([vs, vs], axis=1)
        kwin = _rep2(kw_ref[...].astype(F32), hk).astype(BF16)
        kwr_ref[...] = jnp.concatenate([kwin, kwin], axis=1)
        vwin = _rep2(vw_ref[...].astype(F32), hk).astype(BF16)
        vwr_ref[...] = jnp.concatenate([vwin, vwin], axis=1)
        kc = kc_ref[...]
        kc = (kc + pltpu.roll(kc, HEAD_DIM, 1)).astype(BF16)
        kcr_ref[...] = jnp.concatenate([kc, kc], axis=1)
        vc = vc_ref[...]
        vc = (vc + pltpu.roll(vc, HEAD_DIM, 1)).astype(BF16)
        vcr_ref[...] = jnp.concatenate([vc, vc], axis=1)

    q = q_ref[...]
    zero = jnp.zeros((), BF16)
    masks = [_head_mask(g, w) for g in range(4)]
    tpos = t * TQ + lax.broadcasted_iota(jnp.int32, (TQ, 1), 0)

    ci = lax.broadcasted_iota(jnp.int32, (1, n_crow), 1)
    cmask = jnp.logical_and(ci * NSA_CMP_STRIDE + NSA_CMP_LEN - 1 <= tpos, ci < n_cmp)
    kcr, vcr = kcr_ref[...], vcr_ref[...]
    p_sum = jnp.zeros((TQ, n_crow), F32)
    o_cmp = jnp.zeros((TQ, w), F32)
    for g in range(4):
        s = jnp.where(cmask, _dot_nt(jnp.where(masks[g], q, zero), kcr), NEG)
        m = jnp.max(s, axis=-1, keepdims=True)
        e = jnp.where(cmask, jnp.exp(s - m), 0.0)
        p = e / jnp.maximum(jnp.sum(e, axis=-1, keepdims=True), TINY)
        p_sum = p_sum + p
        o_cmp = jnp.where(masks[g], _dot(p.astype(BF16), vcr), o_cmp)

    c0 = lax.broadcasted_iota(jnp.int32, (n_slc, n_crow), 1) * NSA_CMP_STRIDE
    s0 = lax.broadcasted_iota(jnp.int32, (n_slc, n_crow), 0) * NSA_SLC_BLOCK
    ov = jnp.minimum(c0 + NSA_CMP_LEN, s0 + NSA_SLC_BLOCK) - jnp.maximum(c0, s0)
    ov = jnp.maximum(ov, 0).astype(F32) * (1.0 / NSA_CMP_LEN)
    imp = _dot_nt(ov, p_sum, precision=lax.Precision.HIGHEST)
    j_iota = lax.broadcasted_iota(jnp.int32, (n_slc, TQ), 0)
    q_blk = (t * TQ + lax.broadcasted_iota(jnp.int32, (n_slc, TQ), 1)) // NSA_SLC_BLOCK
    forced = jnp.logical_or(j_iota == 0, jnp.logical_or(j_iota == q_blk, j_iota == q_blk - 1))
    imp = jnp.where(forced, jnp.inf, imp)
    imp = jnp.where(j_iota <= q_blk, imp, -jnp.inf)
    cnt = _rank_before(imp, n_slc)
    allowed = jnp.logical_and(cnt < min(NSA_SLC_TOPN, n_slc), j_iota <= q_blk)
    bias = _bias_lanes(jnp.where(allowed, 1.0, 0.0), SEL_LANE0)

    o_slc = jnp.zeros((TQ, w), F32)
    for g in range(4):
        half_mask = _head_mask(g % 2, LANES)
        if g < 2:
            qaug = jnp.concatenate([jnp.where(half_mask, q[:, :LANES], zero), bias], axis=1)
            kref = ksa_ref
        else:
            qaug = jnp.concatenate([bias, jnp.where(half_mask, q[:, LANES:], zero)], axis=1)
            kref = ksb_ref
        o = _flash_causal(
            qaug,
            lambda j, kref=kref: kref[pl.ds(pl.multiple_of(j * TQ, TQ), TQ), :],
            lambda j: vsr_ref[pl.ds(pl.multiple_of(j * TQ, TQ), TQ), :],
            t, w)
        o_slc = jnp.where(masks[g], o, o_slc)

    k0 = pl.multiple_of(jnp.maximum(t * TQ - NSA_WINDOW, 0), TQ)
    kwin = kwr_ref[pl.ds(k0, NSA_WIN_SPAN), :]
    vwin = vwr_ref[pl.ds(k0, NSA_WIN_SPAN), :]
    dist = tpos - (k0 + lax.broadcasted_iota(jnp.int32, (1, NSA_WIN_SPAN), 1))
    wmask = jnp.logical_and(dist >= 0, dist < NSA_WINDOW)
    o_win = jnp.zeros((TQ, w), F32)
    for g in range(4):
        s = jnp.where(wmask, _dot_nt(jnp.where(masks[g], q, zero), kwin), NEG)
        m = jnp.max(s, axis=-1, keepdims=True)
        e = jnp.where(wmask, jnp.exp(s - m), 0.0)
        p = e / jnp.maximum(jnp.sum(e, axis=-1, keepdims=True), TINY)
        o_win = jnp.where(masks[g], _dot(p.astype(BF16), vwin), o_win)

    gates = jax.nn.sigmoid(g_ref[...].astype(F32))
    er = lax.broadcasted_iota(jnp.int32, (LANES, 3 * w), 0)
    ec = lax.broadcasted_iota(jnp.int32, (LANES, 3 * w), 1)
    expand = jnp.where(er == (ec // w) * NSA_HEADS + hk * 4 + (ec % w) // HEAD_DIM, 1.0, 0.0)
    gx = _dot(gates, expand, precision=lax.Precision.HIGHEST)
    o = gx[:, :w] * o_cmp + gx[:, w:2 * w] * o_slc + gx[:, 2 * w:] * o_win
    o_ref[...] = (o * _silu(z_ref[...].astype(F32))).astype(BF16)


def _nsa(slab, cmp_kv, batch, seq):
    nt = seq // TQ
    w = 4 * HEAD_DIM
    n_crow = cmp_kv.shape[2]
    kv_spec = lambda name: pl.BlockSpec((seq, LANES), lambda b, h, t: (b, _cb(name)))
    return pl.pallas_call(
        functools.partial(_nsa_kernel, seq=seq),
        grid=(batch, NSA_KV_HEADS, nt),
        in_specs=[
            pl.BlockSpec((TQ, w), lambda b, h, t: (b * nt + t, _cb('nsa_q', w) + h)),
            kv_spec('nsa_ks'), kv_spec('nsa_vs'), kv_spec('nsa_kw'), kv_spec('nsa_vw'),
            pl.BlockSpec((None, None, n_crow, LANES), lambda b, h, t: (h, b, 0, 0)),
            pl.BlockSpec((None, None, n_crow, LANES), lambda b, h, t: (NSA_KV_HEADS + h, b, 0, 0)),
            pl.BlockSpec((TQ, LANES), lambda b, h, t: (b * nt + t, _cb('nsa_g'))),
            pl.BlockSpec((TQ, w), lambda b, h, t: (b * nt + t, _cb('nsa_z', w) + h)),
        ],
        out_specs=pl.BlockSpec((TQ, w), lambda b, h, t: (b * nt + t, h)),
        out_shape=jax.ShapeDtypeStruct((batch * seq, NSA_HEADS * HEAD_DIM), BF16),
        scratch_shapes=[
            pltpu.VMEM((seq, w), BF16), pltpu.VMEM((seq, w), BF16), pltpu.VMEM((seq, w), BF16),
            pltpu.VMEM((seq, w), BF16), pltpu.VMEM((seq, w), BF16),
            pltpu.VMEM((n_crow, w), BF16), pltpu.VMEM((n_crow, w), BF16),
        ],
        compiler_params=pltpu.CompilerParams(dimension_semantics=("parallel", "parallel", "arbitrary")),
        name="nsa",
    )(slab, slab, slab, slab, slab, cmp_kv, cmp_kv, slab, slab)


def _merge_kernel(x_ref, g_ref, oa_ref, ob_ref, oc_ref, od_ref, wb_ref, wo_ref, fg_ref, o_ref, *, final):
    merged = None
    row = 0
    for i, br in enumerate((oa_ref, ob_ref, oc_ref, od_ref)):
        width = br.shape[1]
        y = _dot(br[...], wb_ref[row:row + width, :])
        gate = jax.nn.sigmoid(g_ref[:, i * D_MODEL:(i + 1) * D_MODEL].astype(F32))
        merged = gate * y if merged is None else merged + gate * y
        row += width
    x = x_ref[...] + _dot(merged.astype(BF16), wo_ref[...])
    if final:
        ms = jnp.mean(x * x, axis=-1, keepdims=True)
        x = x * lax.rsqrt(ms + NORM_EPS) * fg_ref[...]
    o_ref[...] = x


def _merge(x2, slab, outs, w_branch, w_out, final_g, final):
    t = x2.shape[0]
    tm = MERGE_TM
    d_branch = w_branch.shape[0]
    row_spec = lambda width: pl.BlockSpec((tm, width), lambda i: (i, 0))
    return pl.pallas_call(
        functools.partial(_merge_kernel, final=final),
        grid=(t // tm,),
        in_specs=[
            row_spec(D_MODEL),
            pl.BlockSpec((tm, N_BRANCH * D_MODEL), lambda i: (i, _cb('merge_g', N_BRANCH * D_MODEL))),
            row_spec(outs[0].shape[1]), row_spec(outs[1].shape[1]),
            row_spec(outs[2].shape[1]), row_spec(outs[3].shape[1]),
            pl.BlockSpec((d_branch, D_MODEL), lambda i: (0, 0)),
            pl.BlockSpec((D_MODEL, D_MODEL), lambda i: (0, 0)),
            pl.BlockSpec((1, D_MODEL), lambda i: (0, 0)),
        ],
        out_specs=row_spec(D_MODEL),
        out_shape=jax.ShapeDtypeStruct((t, D_MODEL), F32),
        compiler_params=pltpu.CompilerParams(dimension_semantics=("parallel",)),
        name="merge_out",
    )(x2, slab, *outs, w_branch, w_out, final_g.reshape(1, D_MODEL))


def _rope_tables(positions):
    inv = 1.0 / (ROPE_THETA ** (jnp.arange(0, HEAD_DIM, 2, dtype=F32) / HEAD_DIM))
    ang = positions.astype(F32)[:, None] * inv[None, :]
    cos, sin = jnp.cos(ang), jnp.sin(ang)
    zero = jnp.zeros_like(sin)
    cos_t = jnp.tile(cos, (1, 4))
    sa_t = jnp.tile(jnp.concatenate([-sin, zero], axis=1), (1, 2))
    sb_t = jnp.tile(jnp.concatenate([zero, sin], axis=1), (1, 2))
    return cos_t, sa_t, sb_t


def kernel(x, norm_g, w_in, w_branch, w_out, swa_sink, nsa_cmp_pos, nsa_w_ck1, nsa_w_ck2,
           nsa_w_cv1, nsa_w_cv2, final_norm_g):
    batch, seq, d = x.shape
    depth = w_in.shape[0]
    assert d == D_MODEL and seq % TQ == 0 and seq % RET_CHUNK == 0 and (batch * seq) % MERGE_TM == 0
    assert seq >= NSA_WIN_SPAN and seq % (8 * NSA_CMP_STRIDE) == 0

    cos_t, sa_t, sb_t = _rope_tables(jnp.arange(seq))
    n_crow = seq // NSA_CMP_STRIDE
    cos_c, sa_c, sb_c = _rope_tables(jnp.arange(n_crow) * NSA_CMP_STRIDE + NSA_CMP_LEN - 1)

    idx = jnp.asarray(np.maximum(_SLAB_IDX, 0))
    col_scale = jnp.asarray(_SLAB_SCALE)
    x2 = x.reshape(batch * seq, d)
    for l in range(depth):
        w_slab = (jnp.take(w_in[l], idx, axis=1) * col_scale[None, :]).astype(BF16)
        slab = _proj_in(x2, norm_g[l], w_slab, cos_t, sa_t, sb_t, seq)

        c0 = COL['nsa_kc']
        kcvc = slab[:, c0:c0 + 2 * LANES].reshape(batch, n_crow, NSA_CMP_STRIDE, 2 * NSA_KV_HEADS, HEAD_DIM)
        groups = kcvc.transpose(3, 0, 1, 2, 4).reshape(2 * NSA_KV_HEADS, batch, n_crow, NSA_CMP_STRIDE * HEAD_DIM)
        w1 = jnp.stack([nsa_w_ck1[l], nsa_w_cv1[l]]).astype(BF16)
        w2 = jnp.pad(jnp.stack([nsa_w_ck2[l], nsa_w_cv2[l]]), ((0, 0), (0, 0), (0, LANES - HEAD_DIM))).astype(BF16)
        pos8 = jnp.broadcast_to(nsa_cmp_pos[l].reshape(1, -1), (8, NSA_CMP_LEN * HEAD_DIM)).astype(BF16)
        cmp_kv = _compress(groups, w1, w2, pos8, cos_c, sa_c, sb_c, batch)

        outs = (
            _moba(slab, batch, seq),
            _swa(slab, swa_sink[l], batch, seq),
            _retention(slab, batch, seq),
            _nsa(slab, cmp_kv, batch, seq),
        )
        x2 = _merge(x2, slab, outs, w_branch[l].astype(BF16), w_out[l].astype(BF16),
                    final_norm_g, final=(l == depth - 1))
    return x2.reshape(batch, seq, d)
```

```python
import functools
import math

import numpy as np
import jax
import jax.numpy as jnp
from jax import lax
from jax.experimental import pallas as pl
from jax.experimental.pallas import tpu as pltpu

F32 = jnp.float32
BF16 = jnp.bfloat16

D_MODEL = 1024
HEAD_DIM = 64
ROPE_THETA = 10000.0
NORM_EPS = 1e-6
TINY = 1e-30
N_BRANCH = 4
NEG = -1e30
LOG2E = 1.4426950408889634

MOBA_HEADS = 8
MOBA_BLOCK = 256
MOBA_TOPK = 3

SWA_HEADS = 8
SWA_KV_HEADS = 2
SWA_WINDOW = 128

RET_HEADS = 4
RET_QK_DIM = 64
RET_V_DIM = 128

NSA_HEADS = 8
NSA_KV_HEADS = 2
NSA_CMP_LEN = 32
NSA_CMP_STRIDE = 16
NSA_CMP_HIDDEN = 256
NSA_SLC_BLOCK = 64
NSA_SLC_TOPN = 16
NSA_WINDOW = 512

LANES = 128
TQ = 256
PROJ_TM = 1024
PROJ_TN = 512
MERGE_TM = 512
RET_CHUNK = 256

_IN_SPLITS = (
    ('moba_q', 512), ('moba_k', 512), ('moba_v', 512), ('moba_z', 512),
    ('swa_q', 512), ('swa_k', 128), ('swa_v', 128), ('swa_z', 512),
    ('ret_q', 256), ('ret_k', 256), ('ret_v', 512), ('ret_z', 512),
    ('nsa_q', 512), ('nsa_kc', 128), ('nsa_vc', 128),
    ('nsa_ks', 128), ('nsa_vs', 128), ('nsa_kw', 128), ('nsa_vw', 128),
    ('nsa_g', 24), ('nsa_z', 512),
    ('merge_g', 4096),
)
_WIDTH = dict(_IN_SPLITS)
_SLAB = (
    ('merge_g', 4096),
    ('moba_q', 512), ('moba_k', 512), ('swa_q', 512), ('nsa_q', 512),
    ('ret_q', 256), ('ret_k', 256),
    ('swa_k', 128), ('nsa_ks', 128), ('nsa_kw', 128), (None, 128),
    ('moba_v', 512), ('moba_z', 512), ('swa_z', 512), ('ret_v', 512), ('ret_z', 512), ('nsa_z', 512),
    ('swa_v', 128), ('nsa_vs', 128), ('nsa_vw', 128), ('nsa_g', 128),
    ('nsa_kc', 128), ('nsa_vc', 128), (None, 256),
)
ROPE_TILE_LO, ROPE_TILE_HI = 8, 14
_COL_SCALE = {'moba_q': 0.125 * LOG2E, 'swa_q': 0.125 * LOG2E, 'nsa_q': 0.125 * LOG2E, 'ret_k': 0.125}


def _slab_layout():
    src_off, off = {}, 0
    for name, w in _IN_SPLITS:
        src_off[name] = off
        off += w
    idx, scale, col = [], [], {}
    pos = 0
    for name, w in _SLAB:
        if name is not None:
            col[name] = pos
            real = _WIDTH[name]
            idx += list(range(src_off[name], src_off[name] + real)) + [-1] * (w - real)
            scale += [_COL_SCALE.get(name, 1.0)] * real + [0.0] * (w - real)
        else:
            idx += [-1] * w
            scale += [0.0] * w
        pos += w
    return np.asarray(idx, np.int32), np.asarray(scale, np.float32), col, pos


_SLAB_IDX, _SLAB_SCALE, COL, D_SLAB = _slab_layout()


def _cb(name, width=LANES):
    assert COL[name] % width == 0
    return COL[name] // width


NT = (((1,), (1,)), ((), ()))


def _dot(a, b):
    return jnp.dot(a, b, preferred_element_type=F32)


def _dot_nt(a, b):
    return lax.dot_general(a, b, NT, preferred_element_type=F32)


def _silu(z):
    return z * jax.nn.sigmoid(z)


def _rope_chunk(y, cos, sin_a, sin_b):
    return y * cos + pltpu.roll(y, 96, 1) * sin_a + pltpu.roll(y, 32, 1) * sin_b


def _split3(x):
    x1 = x.astype(BF16).astype(F32)
    x2 = (x - x1).astype(BF16).astype(F32)
    x3 = (x - x1 - x2).astype(BF16).astype(F32)
    return x1, x2, x3


def _proj_kernel(x_ref, g_ref, w_ref, cos_ref, sa_ref, sb_ref, o_ref, h_ref):
    j = pl.program_id(1)

    @pl.when(j == 0)
    def _():
        x = x_ref[...]
        ms = jnp.mean(x * x, axis=-1, keepdims=True)
        h_ref[...] = (x * lax.rsqrt(ms + NORM_EPS) * g_ref[...]).astype(BF16)

    y = _dot(h_ref[...], w_ref[...])
    is_rope = jnp.logical_and(j >= ROPE_TILE_LO, j < ROPE_TILE_HI)

    @pl.when(is_rope)
    def _():
        cos, sa, sb = cos_ref[...], sa_ref[...], sb_ref[...]
        for c in range(PROJ_TN // LANES):
            sl = slice(c * LANES, (c + 1) * LANES)
            o_ref[:, sl] = _rope_chunk(y[:, sl], cos, sa, sb).astype(BF16)

    @pl.when(jnp.logical_not(is_rope))
    def _():
        o_ref[...] = y.astype(BF16)


def _proj_in(x2, norm_g, w_slab, cos_t, sa_t, sb_t, seq):
    t = x2.shape[0]
    tm = min(PROJ_TM, seq)
    per_seq = seq // tm
    return pl.pallas_call(
        _proj_kernel,
        grid=(t // tm, D_SLAB // PROJ_TN),
        in_specs=[
            pl.BlockSpec((tm, D_MODEL), lambda i, j: (i, 0)),
            pl.BlockSpec((1, D_MODEL), lambda i, j: (0, 0)),
            pl.BlockSpec((D_MODEL, PROJ_TN), lambda i, j: (0, j)),
            pl.BlockSpec((tm, LANES), lambda i, j: (i % per_seq, 0)),
            pl.BlockSpec((tm, LANES), lambda i, j: (i % per_seq, 0)),
            pl.BlockSpec((tm, LANES), lambda i, j: (i % per_seq, 0)),
        ],
        out_specs=pl.BlockSpec((tm, PROJ_TN), lambda i, j: (i, j)),
        out_shape=jax.ShapeDtypeStruct((t, D_SLAB), BF16),
        scratch_shapes=[pltpu.VMEM((tm, D_MODEL), BF16)],
        compiler_params=pltpu.CompilerParams(dimension_semantics=("parallel", "arbitrary")),
        name="proj_in",
    )(x2, norm_g.reshape(1, D_MODEL), w_slab, cos_t, sa_t, sb_t)


def _rank_before(scores, n):
    j_iota = lax.broadcasted_iota(jnp.int32, scores.shape, 0)
    cnt = jnp.zeros(scores.shape, F32)
    for jp in range(n):
        r = scores[jp:jp + 1, :]
        ahead = jnp.logical_or(r > scores, jnp.logical_and(r == scores, jp < j_iota))
        cnt = cnt + jnp.where(ahead, 1.0, 0.0)
    return cnt


def _normalize(o, ones_lane):
    return o * (1.0 / jnp.maximum(o[:, ones_lane:ones_lane + 1], TINY))


def _prefix_attention(qaug, k_at, v_at, n_tiles, ones_lane):
    n_past = (n_tiles - 1) * TQ
    row = lax.broadcasted_iota(jnp.int32, (TQ, TQ), 0)
    col = lax.broadcasted_iota(jnp.int32, (TQ, TQ), 1)
    s_d = jnp.where(col <= row, _dot_nt(qaug, k_at(n_past, TQ)), NEG)
    m = jnp.max(s_d, axis=-1, keepdims=True)
    if n_past:
        s_p = _dot_nt(qaug, k_at(0, n_past))
        m = jnp.maximum(m, jnp.max(s_p, axis=-1, keepdims=True))
    o = _dot(jnp.exp2(s_d - m).astype(BF16), v_at(n_past, TQ))
    if n_past:
        o = o + _dot(jnp.exp2(s_p - m).astype(BF16), v_at(0, n_past))
    return _normalize(o, ones_lane)


def _bias_lanes(allowed_t, row0):
    n, rows = allowed_t.shape
    pieces = []
    if row0:
        pieces.append(jnp.zeros((row0, rows), F32))
    pieces.append(allowed_t)
    if LANES - row0 - n:
        pieces.append(jnp.zeros((LANES - row0 - n, rows), F32))
    full = jnp.concatenate(pieces, axis=0)
    return ((full.T - 1.0) * (-NEG)).astype(BF16)


def _moba_kernel(q_ref, k_ref, v_ref, z_ref, o_ref, kaug_ref, vaug_ref, kms_ref, *, seq):
    t = pl.program_id(1)
    nb = seq // MOBA_BLOCK
    n_pair = MOBA_HEADS // 2

    @pl.when(t == 0)
    def _():
        rowblk = lax.broadcasted_iota(jnp.int32, (seq, LANES), 0) // MOBA_BLOCK
        lane = lax.broadcasted_iota(jnp.int32, (seq, LANES), 1)
        low = lane < HEAD_DIM
        blk = lax.broadcasted_iota(jnp.int32, (nb, seq), 0)
        pos = lax.broadcasted_iota(jnp.int32, (nb, seq), 1) // MOBA_BLOCK
        avg = jnp.where(blk == pos, 1.0 / MOBA_BLOCK, 0.0).astype(BF16)
        for p in range(n_pair):
            sl = slice(p * LANES, (p + 1) * LANES)
            kf = k_ref[:, sl].astype(F32)
            vf = v_ref[:, sl].astype(F32)
            kaug_ref[2 * p] = jnp.where(low, kf, jnp.where(lane - HEAD_DIM == rowblk, 1.0, 0.0)).astype(BF16)
            kaug_ref[2 * p + 1] = jnp.where(low, jnp.where(lane == rowblk, 1.0, 0.0), kf).astype(BF16)
            vaug_ref[2 * p] = jnp.where(low, vf, 1.0).astype(BF16)
            vaug_ref[2 * p + 1] = jnp.where(low, 1.0, vf).astype(BF16)
            kms_ref[p] = jnp.concatenate(_split3(_dot(avg, k_ref[:, sl])), axis=0)

    lane = lax.broadcasted_iota(jnp.int32, (TQ, LANES), 1)
    j_iota = lax.broadcasted_iota(jnp.int32, (nb, TQ), 0)
    qaugs = []
    for p in range(n_pair):
        q = q_ref[:, p * LANES:(p + 1) * LANES]
        kms = kms_ref[p].astype(BF16)
        for h in range(2):
            mine = (lane < HEAD_DIM) if h == 0 else (lane >= HEAD_DIM)
            g3 = _dot_nt(kms, jnp.where(mine, q, jnp.zeros((), BF16)))
            gate = g3[:nb] + g3[nb:2 * nb] + g3[2 * nb:]
            gate = jnp.where(j_iota < t, gate, -jnp.inf)
            cnt = _rank_before(gate, nb)
            allowed = jnp.logical_or(jnp.logical_and(cnt < MOBA_TOPK, j_iota < t), j_iota == t)
            bias = _bias_lanes(jnp.where(allowed, 1.0, 0.0), HEAD_DIM if h == 0 else 0)
            qaugs.append(jnp.where(mine, q, bias))
    gate_z = _silu(z_ref[...].astype(F32))

    for tt in range(nb):
        @pl.when(t == tt)
        def _(tt=tt):
            for p in range(n_pair):
                outs = [
                    _prefix_attention(
                        qaugs[2 * p + h],
                        lambda start, size, i=2 * p + h: kaug_ref[i, start:start + size, :],
                        lambda start, size, i=2 * p + h: vaug_ref[i, start:start + size, :],
                        tt + 1, HEAD_DIM if h == 0 else 0)
                    for h in range(2)
                ]
                sl = slice(p * LANES, (p + 1) * LANES)
                o_ref[:, sl] = (jnp.where(lane < HEAD_DIM, outs[0], outs[1]) * gate_z[:, sl]).astype(BF16)


def _moba(slab, batch, seq):
    nt = seq // TQ
    w = MOBA_HEADS * HEAD_DIM
    nb = seq // MOBA_BLOCK
    return pl.pallas_call(
        functools.partial(_moba_kernel, seq=seq),
        grid=(batch, nt),
        in_specs=[
            pl.BlockSpec((TQ, w), lambda b, t: (b * nt + t, _cb('moba_q', w))),
            pl.BlockSpec((seq, w), lambda b, t: (b, _cb('moba_k', w))),
            pl.BlockSpec((seq, w), lambda b, t: (b, _cb('moba_v', w))),
            pl.BlockSpec((TQ, w), lambda b, t: (b * nt + t, _cb('moba_z', w))),
        ],
        out_specs=pl.BlockSpec((TQ, w), lambda b, t: (b * nt + t, 0)),
        out_shape=jax.ShapeDtypeStruct((batch * seq, w), BF16),
        scratch_shapes=[
            pltpu.VMEM((MOBA_HEADS, seq, LANES), BF16),
            pltpu.VMEM((MOBA_HEADS, seq, LANES), BF16),
            pltpu.VMEM((MOBA_HEADS // 2, 3 * nb, LANES), F32),
        ],
        compiler_params=pltpu.CompilerParams(dimension_semantics=("parallel", "arbitrary")),
        name="moba",
    )(slab, slab, slab, slab)


GQA_W = 4 * HEAD_DIM
ONES_A, ONES_B = 3 * HEAD_DIM, 0


def _rep2(x, hk):
    lane = lax.broadcasted_iota(jnp.int32, x.shape, 1)
    keep = (lane < HEAD_DIM) == (hk == 0)
    return jnp.where(keep, x, pltpu.roll(x, HEAD_DIM, 1))


def _value_variants(v2):
    low = lax.broadcasted_iota(jnp.int32, v2.shape, 1) < HEAD_DIM
    va = jnp.concatenate([v2, jnp.where(low, v2, 1.0)], axis=1).astype(BF16)
    vb = jnp.concatenate([jnp.where(low, 1.0, v2), v2], axis=1).astype(BF16)
    return va, vb


def _head_mask(g, width, rows=TQ):
    lane = lax.broadcasted_iota(jnp.int32, (rows, width), 1)
    return (lane // HEAD_DIM) == g


SWA_SPAN = TQ + LANES


def _swa_kernel(sink_ref, q_ref, k_ref, v_ref, z_ref, o_ref, krep_ref, va_ref, vb_ref):
    hk = pl.program_id(1)
    t = pl.program_id(2)

    @pl.when(t == 0)
    def _():
        kr = _rep2(k_ref[...].astype(F32), hk).astype(BF16)
        krep_ref[...] = jnp.concatenate([kr, kr], axis=1)
        va_ref[...], vb_ref[...] = _value_variants(_rep2(v_ref[...].astype(F32), hk))

    q = q_ref[...]
    k0 = pl.multiple_of(jnp.maximum(t * TQ - LANES, 0), LANES)
    kw = krep_ref[pl.ds(k0, SWA_SPAN), :]
    dist = (t * TQ + lax.broadcasted_iota(jnp.int32, (TQ, SWA_SPAN), 0)) - (
        k0 + lax.broadcasted_iota(jnp.int32, (TQ, SWA_SPAN), 1))
    mask = jnp.logical_and(dist >= 0, dist < SWA_WINDOW)
    out = jnp.zeros((TQ, GQA_W), F32)
    for g in range(4):
        mine = _head_mask(g, GQA_W)
        vref, ones = (va_ref, ONES_A) if g < 3 else (vb_ref, ONES_B)
        s = jnp.where(mask, _dot_nt(jnp.where(mine, q, jnp.zeros((), BF16)), kw), NEG)
        sink = sink_ref[hk * 4 + g] * LOG2E
        m = jnp.maximum(jnp.max(s, axis=-1, keepdims=True), sink)
        o = _dot(jnp.exp2(s - m).astype(BF16), vref[pl.ds(k0, SWA_SPAN), :])
        den = o[:, ones:ones + 1] + jnp.exp2(sink - m)
        out = jnp.where(mine, o * (1.0 / jnp.maximum(den, TINY)), out)
    o_ref[...] = (out * _silu(z_ref[...].astype(F32))).astype(BF16)


def _swa(slab, sink, batch, seq):
    nt = seq // TQ
    w = GQA_W
    return pl.pallas_call(
        _swa_kernel,
        grid=(batch, SWA_KV_HEADS, nt),
        in_specs=[
            pl.BlockSpec(memory_space=pltpu.SMEM),
            pl.BlockSpec((TQ, w), lambda b, h, t: (b * nt + t, _cb('swa_q', w) + h)),
            pl.BlockSpec((seq, LANES), lambda b, h, t: (b, _cb('swa_k'))),
            pl.BlockSpec((seq, LANES), lambda b, h, t: (b, _cb('swa_v'))),
            pl.BlockSpec((TQ, w), lambda b, h, t: (b * nt + t, _cb('swa_z', w) + h)),
        ],
        out_specs=pl.BlockSpec((TQ, w), lambda b, h, t: (b * nt + t, h)),
        out_shape=jax.ShapeDtypeStruct((batch * seq, SWA_HEADS * HEAD_DIM), BF16),
        scratch_shapes=[pltpu.VMEM((seq, w), BF16), pltpu.VMEM((seq, w), BF16), pltpu.VMEM((seq, w), BF16)],
        compiler_params=pltpu.CompilerParams(dimension_semantics=("parallel", "parallel", "arbitrary")),
        name="swa",
    )(sink, slab, slab, slab, slab)


def _ret_kernel(qk_ref, v_ref, z_ref, o_ref, state_ref):
    c = RET_CHUNK

    @pl.when(pl.program_id(1) == 0)
    def _():
        state_ref[...] = jnp.zeros(state_ref.shape, F32)

    q = qk_ref[:, :RET_HEADS * RET_QK_DIM]
    k = qk_ref[:, RET_HEADS * RET_QK_DIM:]
    ii = lax.broadcasted_iota(jnp.int32, (c, c), 0)
    jj = lax.broadcasted_iota(jnp.int32, (c, c), 1)
    diff = (ii - jj).astype(F32)
    row = lax.broadcasted_iota(jnp.int32, (c, 1), 0).astype(F32)
    lane = lax.broadcasted_iota(jnp.int32, (c, RET_HEADS * RET_QK_DIM), 1)
    for h in range(RET_HEADS):
        log_g = math.log(1.0 - 2.0 ** (-5.0 - h))
        intra = jnp.where(diff >= 0, jnp.exp(jnp.maximum(diff, 0.0) * log_g), 0.0)
        q_dec = jnp.exp((row + 1.0) * log_g)
        k_dec = jnp.exp((c - 1.0 - row) * log_g)
        chunk_dec = math.exp(c * log_g)
        qm = jnp.where((lane // RET_QK_DIM) == h, q, jnp.zeros((), BF16))
        vh = v_ref[:, h * RET_V_DIM:(h + 1) * RET_V_DIM]
        att = _dot_nt(qm, k) * intra
        st = state_ref[h]
        o = _dot(att.astype(BF16), vh) + _dot(qm, st.astype(BF16)) * q_dec
        kd = (k.astype(F32) * k_dec).T.astype(BF16)
        state_ref[h] = st * chunk_dec + _dot(kd, vh)
        mu = jnp.mean(o, axis=-1, keepdims=True)
        var = jnp.mean(jnp.square(o - mu), axis=-1, keepdims=True)
        o = (o - mu) * lax.rsqrt(var + NORM_EPS)
        zh = z_ref[:, h * RET_V_DIM:(h + 1) * RET_V_DIM].astype(F32)
        o_ref[:, h * RET_V_DIM:(h + 1) * RET_V_DIM] = (o * _silu(zh)).astype(BF16)


def _retention(slab, batch, seq):
    nc = seq // RET_CHUNK
    w = RET_HEADS * RET_V_DIM
    return pl.pallas_call(
        _ret_kernel,
        grid=(batch, nc),
        in_specs=[
            pl.BlockSpec((RET_CHUNK, w), lambda b, c: (b * nc + c, _cb('ret_q', w))),
            pl.BlockSpec((RET_CHUNK, w), lambda b, c: (b * nc + c, _cb('ret_v', w))),
            pl.BlockSpec((RET_CHUNK, w), lambda b, c: (b * nc + c, _cb('ret_z', w))),
        ],
        out_specs=pl.BlockSpec((RET_CHUNK, w), lambda b, c: (b * nc + c, 0)),
        out_shape=jax.ShapeDtypeStruct((batch * seq, w), BF16),
        scratch_shapes=[pltpu.VMEM((RET_HEADS, RET_HEADS * RET_QK_DIM, RET_V_DIM), F32)],
        compiler_params=pltpu.CompilerParams(dimension_semantics=("parallel", "arbitrary")),
        name="retention",
    )(slab, slab, slab)


def _compress_kernel(r_ref, w1_ref, w2_ref, pos_ref, cos_ref, sa_ref, sb_ref, o_ref):
    kind = pl.program_id(0) // NSA_KV_HEADS
    half = NSA_CMP_STRIDE * HEAD_DIM
    r = r_ref[...]
    n_rows = r.shape[0]
    lo = _dot(r, w1_ref[:half, :])
    hi = _dot(r, w1_ref[half:, :])
    pos = _dot(pos_ref[...], w1_ref[...])[0:1, :]
    pre = lo + pltpu.roll(hi, n_rows - 1, 0) + pos
    y = _dot(jax.nn.gelu(pre).astype(BF16), w2_ref[...])
    roped = _rope_chunk(y, cos_ref[...], sa_ref[...], sb_ref[...])
    o_ref[...] = jnp.where(kind == 0, roped, y)


def _compress(r, w1, w2, pos8, cos_c, sa_c, sb_c, batch):
    n_rows = r.shape[2]
    flat = NSA_CMP_LEN * HEAD_DIM
    return pl.pallas_call(
        _compress_kernel,
        grid=(2 * NSA_KV_HEADS, batch),
        in_specs=[
            pl.BlockSpec((None, None, n_rows, flat // 2), lambda i, b: (i, b, 0, 0)),
            pl.BlockSpec((None, flat, NSA_CMP_HIDDEN), lambda i, b: (i // NSA_KV_HEADS, 0, 0)),
            pl.BlockSpec((None, NSA_CMP_HIDDEN, LANES), lambda i, b: (i // NSA_KV_HEADS, 0, 0)),
            pl.BlockSpec((8, flat), lambda i, b: (0, 0)),
            pl.BlockSpec((n_rows, LANES), lambda i, b: (0, 0)),
            pl.BlockSpec((n_rows, LANES), lambda i, b: (0, 0)),
            pl.BlockSpec((n_rows, LANES), lambda i, b: (0, 0)),
        ],
        out_specs=pl.BlockSpec((None, None, n_rows, LANES), lambda i, b: (i, b, 0, 0)),
        out_shape=jax.ShapeDtypeStruct((2 * NSA_KV_HEADS, batch, n_rows, LANES), F32),
        compiler_params=pltpu.CompilerParams(dimension_semantics=("parallel", "parallel")),
        name="nsa_compress",
    )(r, w1, w2, pos8, cos_c, sa_c, sb_c)


NSA_WIN_SPAN = TQ + NSA_WINDOW
SEL_LANE0 = 64


def _nsa_kernel(q_ref, ks_ref, vs_ref, kw_ref, vw_ref, kc_ref, vc_ref, g_ref, z_ref, o_ref,
                ksa_ref, ksb_ref, vsa_ref, vsb_ref, kwr_ref, vwa_ref, vwb_ref, kcr_ref, vcr_ref,
                ocmp_ref, bias_ref, oslc_ref, *, seq):
    hk = pl.program_id(1)
    t = pl.program_id(2)
    n_slc = seq // NSA_SLC_BLOCK
    n_cmp = (seq - NSA_CMP_LEN) // NSA_CMP_STRIDE + 1
    n_crow = kc_ref.shape[0]
    w = GQA_W

    @pl.when(t == 0)
    def _():
        ks = _rep2(ks_ref[...].astype(F32), hk).astype(BF16)
        rowblk = lax.broadcasted_iota(jnp.int32, (seq, LANES), 0) // NSA_SLC_BLOCK
        lane = lax.broadcasted_iota(jnp.int32, (seq, LANES), 1)
        onehot = jnp.where(lane - SEL_LANE0 == rowblk, 1.0, 0.0).astype(BF16)
        ksa_ref[...] = jnp.concatenate([ks, onehot], axis=1)
        ksb_ref[...] = jnp.concatenate([onehot, ks], axis=1)
        vsa_ref[...], vsb_ref[...] = _value_variants(_rep2(vs_ref[...].astype(F32), hk))
        kwin = _rep2(kw_ref[...].astype(F32), hk).astype(BF16)
        kwr_ref[...] = jnp.concatenate([kwin, kwin], axis=1)
        vwa_ref[...], vwb_ref[...] = _value_variants(_rep2(vw_ref[...].astype(F32), hk))
        kc = kc_ref[...]
        kc = (kc + pltpu.roll(kc, HEAD_DIM, 1)).astype(BF16)
        kcr_ref[...] = jnp.concatenate([kc, kc], axis=1)
        vc = vc_ref[...]
        vc = (vc + pltpu.roll(vc, HEAD_DIM, 1)).astype(BF16)
        vcr_ref[...] = jnp.concatenate([vc, vc], axis=1)

        q_all = q_ref[...]
        pos_all = lax.broadcasted_iota(jnp.int32, (seq, 1), 0)
        ci = lax.broadcasted_iota(jnp.int32, (1, n_crow), 1)
        cmask = jnp.logical_and(ci * NSA_CMP_STRIDE + NSA_CMP_LEN - 1 <= pos_all, ci < n_cmp)
        kcr, vcr = kcr_ref[...], vcr_ref[...]
        p_sum = jnp.zeros((seq, n_crow), F32)
        o_cmp = jnp.zeros((seq, w), F32)
        for g in range(4):
            mask_g = _head_mask(g, w, seq)
            s = jnp.where(cmask, _dot_nt(jnp.where(mask_g, q_all, jnp.zeros((), BF16)), kcr), NEG)
            m = jnp.max(s, axis=-1, keepdims=True)
            e = jnp.where(cmask, jnp.exp2(s - m), 0.0)
            p = e * (1.0 / jnp.maximum(jnp.sum(e, axis=-1, keepdims=True), TINY))
            p_sum = p_sum + p
            o_cmp = jnp.where(mask_g, _dot(p.astype(BF16), vcr), o_cmp)
        ocmp_ref[...] = o_cmp

        c0 = lax.broadcasted_iota(jnp.int32, (n_slc, n_crow), 1) * NSA_CMP_STRIDE
        s0 = lax.broadcasted_iota(jnp.int32, (n_slc, n_crow), 0) * NSA_SLC_BLOCK
        ov = jnp.minimum(c0 + NSA_CMP_LEN, s0 + NSA_SLC_BLOCK) - jnp.maximum(c0, s0)
        ov = (jnp.maximum(ov, 0).astype(F32) * (1.0 / NSA_CMP_LEN)).astype(BF16)
        imp = _dot_nt(jnp.concatenate([ov, ov, ov], axis=1),
                      jnp.concatenate(_split3(p_sum), axis=1).astype(BF16))
        j_iota = lax.broadcasted_iota(jnp.int32, (n_slc, seq), 0)
        q_blk = lax.broadcasted_iota(jnp.int32, (n_slc, seq), 1) // NSA_SLC_BLOCK
        forced = jnp.logical_or(j_iota == 0, jnp.logical_or(j_iota == q_blk, j_iota == q_blk - 1))
        imp = jnp.where(forced, jnp.inf, imp)
        imp = jnp.where(j_iota <= q_blk, imp, -jnp.inf)
        cnt = _rank_before(imp, n_slc)
        allowed = jnp.logical_and(cnt < min(NSA_SLC_TOPN, n_slc), j_iota <= q_blk)
        bias_ref[...] = _bias_lanes(jnp.where(allowed, 1.0, 0.0), SEL_LANE0)

    rows = pl.ds(pl.multiple_of(t * TQ, TQ), TQ)
    q = q_ref[rows, :]
    bias = bias_ref[rows, :]
    zero = jnp.zeros((), BF16)
    masks = [_head_mask(g, w) for g in range(4)]
    tpos = t * TQ + lax.broadcasted_iota(jnp.int32, (TQ, 1), 0)

    k0 = pl.multiple_of(jnp.maximum(t * TQ - NSA_WINDOW, 0), TQ)
    kwin = kwr_ref[pl.ds(k0, NSA_WIN_SPAN), :]
    dist = tpos - (k0 + lax.broadcasted_iota(jnp.int32, (1, NSA_WIN_SPAN), 1))
    wmask = jnp.logical_and(dist >= 0, dist < NSA_WINDOW)
    o_win = jnp.zeros((TQ, w), F32)
    for g in range(4):
        vref, ones = (vwa_ref, ONES_A) if g < 3 else (vwb_ref, ONES_B)
        s = jnp.where(wmask, _dot_nt(jnp.where(masks[g], q, zero), kwin), NEG)
        m = jnp.max(s, axis=-1, keepdims=True)
        o = _dot(jnp.exp2(s - m).astype(BF16), vref[pl.ds(k0, NSA_WIN_SPAN), :])
        o_win = jnp.where(masks[g], _normalize(o, ones), o_win)

    qaugs = []
    for g in range(4):
        half_mask = _head_mask(g % 2, LANES)
        if g < 2:
            qaugs.append(jnp.concatenate([jnp.where(half_mask, q[:, :LANES], zero), bias], axis=1))
        else:
            qaugs.append(jnp.concatenate([bias, jnp.where(half_mask, q[:, LANES:], zero)], axis=1))

    for tt in range(seq // TQ):
        @pl.when(t == tt)
        def _(tt=tt):
            o_slc = jnp.zeros((TQ, w), F32)
            for g in range(4):
                kref = ksa_ref if g < 2 else ksb_ref
                vref, ones = (vsa_ref, ONES_A) if g < 3 else (vsb_ref, ONES_B)
                o = _prefix_attention(
                    qaugs[g],
                    lambda start, size, kref=kref: kref[start:start + size, :],
                    lambda start, size, vref=vref: vref[start:start + size, :],
                    tt + 1, ones)
                o_slc = jnp.where(masks[g], o, o_slc)
            oslc_ref[...] = o_slc
    o_slc = oslc_ref[...]

    gates = jax.nn.sigmoid(g_ref[...].astype(F32))
    g_hi = gates.astype(BF16)
    g_lo = (gates - g_hi.astype(F32)).astype(BF16)
    er = lax.broadcasted_iota(jnp.int32, (2 * LANES, 3 * w), 0) % LANES
    ec = lax.broadcasted_iota(jnp.int32, (2 * LANES, 3 * w), 1)
    expand = jnp.where(er == (ec // w) * NSA_HEADS + hk * 4 + (ec % w) // HEAD_DIM, 1.0, 0.0).astype(BF16)
    gx = _dot(jnp.concatenate([g_hi, g_lo], axis=1), expand)
    o = gx[:, :w] * ocmp_ref[rows, :] + gx[:, w:2 * w] * o_slc + gx[:, 2 * w:] * o_win
    o_ref[...] = (o * _silu(z_ref[...].astype(F32))).astype(BF16)


def _nsa(slab, cmp_kv, batch, seq):
    nt = seq // TQ
    w = GQA_W
    n_crow = cmp_kv.shape[2]
    kv_spec = lambda name: pl.BlockSpec((seq, LANES), lambda b, h, t: (b, _cb(name)))
    big = pltpu.VMEM((seq, w), BF16)
    return pl.pallas_call(
        functools.partial(_nsa_kernel, seq=seq),
        grid=(batch, NSA_KV_HEADS, nt),
        in_specs=[
            pl.BlockSpec((seq, w), lambda b, h, t: (b, _cb('nsa_q', w) + h)),
            kv_spec('nsa_ks'), kv_spec('nsa_vs'), kv_spec('nsa_kw'), kv_spec('nsa_vw'),
            pl.BlockSpec((None, None, n_crow, LANES), lambda b, h, t: (h, b, 0, 0)),
            pl.BlockSpec((None, None, n_crow, LANES), lambda b, h, t: (NSA_KV_HEADS + h, b, 0, 0)),
            pl.BlockSpec((TQ, LANES), lambda b, h, t: (b * nt + t, _cb('nsa_g'))),
            pl.BlockSpec((TQ, w), lambda b, h, t: (b * nt + t, _cb('nsa_z', w) + h)),
        ],
        out_specs=pl.BlockSpec((TQ, w), lambda b, h, t: (b * nt + t, h)),
        out_shape=jax.ShapeDtypeStruct((batch * seq, NSA_HEADS * HEAD_DIM), BF16),
        scratch_shapes=[
            big, big, big, big, big, big, big,
            pltpu.VMEM((n_crow, w), BF16), pltpu.VMEM((n_crow, w), BF16),
            pltpu.VMEM((seq, w), F32), pltpu.VMEM((seq, LANES), BF16), pltpu.VMEM((TQ, w), F32),
        ],
        compiler_params=pltpu.CompilerParams(dimension_semantics=("parallel", "parallel", "arbitrary")),
        name="nsa",
    )(slab, slab, slab, slab, slab, cmp_kv, cmp_kv, slab, slab)


def _merge_kernel(x_ref, g_ref, oa_ref, ob_ref, oc_ref, od_ref, wb_ref, wo_ref, fg_ref, o_ref, *, final):
    merged = None
    row = 0
    for i, br in enumerate((oa_ref, ob_ref, oc_ref, od_ref)):
        width = br.shape[1]
        y = _dot(br[...], wb_ref[row:row + width, :])
        gate = jax.nn.sigmoid(g_ref[:, i * D_MODEL:(i + 1) * D_MODEL].astype(F32))
        merged = gate * y if merged is None else merged + gate * y
        row += width
    x = x_ref[...] + _dot(merged.astype(BF16), wo_ref[...])
    if final:
        ms = jnp.mean(x * x, axis=-1, keepdims=True)
        x = x * lax.rsqrt(ms + NORM_EPS) * fg_ref[...]
    o_ref[...] = x


def _merge(x2, slab, outs, w_branch, w_out, final_g, final):
    t = x2.shape[0]
    tm = MERGE_TM
    d_branch = w_branch.shape[0]
    row_spec = lambda width: pl.BlockSpec((tm, width), lambda i: (i, 0))
    return pl.pallas_call(
        functools.partial(_merge_kernel, final=final),
        grid=(t // tm,),
        in_specs=[
            row_spec(D_MODEL),
            pl.BlockSpec((tm, N_BRANCH * D_MODEL), lambda i: (i, _cb('merge_g', N_BRANCH * D_MODEL))),
            row_spec(outs[0].shape[1]), row_spec(outs[1].shape[1]),
            row_spec(outs[2].shape[1]), row_spec(outs[3].shape[1]),
            pl.BlockSpec((d_branch, D_MODEL), lambda i: (0, 0)),
            pl.BlockSpec((D_MODEL, D_MODEL), lambda i: (0, 0)),
            pl.BlockSpec((1, D_MODEL), lambda i: (0, 0)),
        ],
        out_specs=row_spec(D_MODEL),
        out_shape=jax.ShapeDtypeStruct((t, D_MODEL), F32),
        compiler_params=pltpu.CompilerParams(dimension_semantics=("parallel",)),
        name="merge_out",
    )(x2, slab, *outs, w_branch, w_out, final_g.reshape(1, D_MODEL))


def _rope_tables(positions):
    inv = 1.0 / (ROPE_THETA ** (jnp.arange(0, HEAD_DIM, 2, dtype=F32) / HEAD_DIM))
    ang = positions.astype(F32)[:, None] * inv[None, :]
    cos, sin = jnp.cos(ang), jnp.sin(ang)
    zero = jnp.zeros_like(sin)
    cos_t = jnp.tile(cos, (1, 4))
    sa_t = jnp.tile(jnp.concatenate([-sin, zero], axis=1), (1, 2))
    sb_t = jnp.tile(jnp.concatenate([zero, sin], axis=1), (1, 2))
    return cos_t, sa_t, sb_t


def kernel(x, norm_g, w_in, w_branch, w_out, swa_sink, nsa_cmp_pos, nsa_w_ck1, nsa_w_ck2,
           nsa_w_cv1, nsa_w_cv2, final_norm_g):
    batch, seq, d = x.shape
    depth = w_in.shape[0]
    assert d == D_MODEL and seq % TQ == 0 and seq % RET_CHUNK == 0 and (batch * seq) % MERGE_TM == 0
    assert seq >= NSA_WIN_SPAN and seq % (8 * NSA_CMP_STRIDE) == 0 and (seq // MOBA_BLOCK) % 8 == 0

    cos_t, sa_t, sb_t = _rope_tables(jnp.arange(seq))
    n_crow = seq // NSA_CMP_STRIDE
    cos_c, sa_c, sb_c = _rope_tables(jnp.arange(n_crow) * NSA_CMP_STRIDE + NSA_CMP_LEN - 1)

    idx = jnp.asarray(np.maximum(_SLAB_IDX, 0))
    col_scale = jnp.asarray(_SLAB_SCALE)
    x2 = x.reshape(batch * seq, d)
    for l in range(depth):
        w_slab = (jnp.take(w_in[l], idx, axis=1) * col_scale[None, :]).astype(BF16)
        slab = _proj_in(x2, norm_g[l], w_slab, cos_t, sa_t, sb_t, seq)

        c0 = COL['nsa_kc']
        kcvc = slab[:, c0:c0 + 2 * LANES].reshape(batch, n_crow, NSA_CMP_STRIDE, 2 * NSA_KV_HEADS, HEAD_DIM)
        groups = kcvc.transpose(3, 0, 1, 2, 4).reshape(2 * NSA_KV_HEADS, batch, n_crow, NSA_CMP_STRIDE * HEAD_DIM)
        w1 = jnp.stack([nsa_w_ck1[l], nsa_w_cv1[l]]).astype(BF16)
        w2 = jnp.pad(jnp.stack([nsa_w_ck2[l], nsa_w_cv2[l]]), ((0, 0), (0, 0), (0, LANES - HEAD_DIM))).astype(BF16)
        pos8 = jnp.broadcast_to(nsa_cmp_pos[l].reshape(1, -1), (8, NSA_CMP_LEN * HEAD_DIM)).astype(BF16)
        cmp_kv = _compress(groups, w1, w2, pos8, cos_c, sa_c, sb_c, batch)

        outs = (
            _moba(slab, batch, seq),
            _swa(slab, swa_sink[l], batch, seq),
            _retention(slab, batch, seq),
            _nsa(slab, cmp_kv, batch, seq),
        )
        x2 = _merge(x2, slab, outs, w_branch[l].astype(BF16), w_out[l].astype(BF16),
                    final_norm_g, final=(l == depth - 1))
    return x2.reshape(batch, seq, d)
```

```python
import functools
import math

import numpy as np
import jax
import jax.numpy as jnp
from jax import lax
from jax.experimental import pallas as pl
from jax.experimental.pallas import tpu as pltpu

F32 = jnp.float32
BF16 = jnp.bfloat16

D_MODEL = 1024
HEAD_DIM = 64
ROPE_THETA = 10000.0
NORM_EPS = 1e-6
TINY = 1e-30
N_BRANCH = 4
NEG = -1e30
LOG2E = 1.4426950408889634

MOBA_HEADS = 8
MOBA_BLOCK = 256
MOBA_TOPK = 3

SWA_HEADS = 8
SWA_KV_HEADS = 2
SWA_WINDOW = 128

RET_HEADS = 4
RET_QK_DIM = 64
RET_V_DIM = 128

NSA_HEADS = 8
NSA_KV_HEADS = 2
NSA_CMP_LEN = 32
NSA_CMP_STRIDE = 16
NSA_CMP_HIDDEN = 256
NSA_SLC_BLOCK = 64
NSA_SLC_TOPN = 16
NSA_WINDOW = 512

LANES = 128
TQ = 256
PROJ_TM = 2048
PROJ_TN = 1024
MERGE_TM = 512
RET_CHUNK = 256

_IN_SPLITS = (
    ('moba_q', 512), ('moba_k', 512), ('moba_v', 512), ('moba_z', 512),
    ('swa_q', 512), ('swa_k', 128), ('swa_v', 128), ('swa_z', 512),
    ('ret_q', 256), ('ret_k', 256), ('ret_v', 512), ('ret_z', 512),
    ('nsa_q', 512), ('nsa_kc', 128), ('nsa_vc', 128),
    ('nsa_ks', 128), ('nsa_vs', 128), ('nsa_kw', 128), ('nsa_vw', 128),
    ('nsa_g', 24), ('nsa_z', 512),
    ('merge_g', 4096),
)
_WIDTH = dict(_IN_SPLITS)
_SLAB = (
    ('merge_g', 4096),
    ('moba_q', 512), ('moba_k', 512), ('swa_q', 512), ('nsa_q', 512),
    ('ret_q', 256), ('ret_k', 256),
    ('swa_k', 128), ('nsa_ks', 128), ('nsa_kw', 128), (None, 128),
    ('moba_v', 512), ('moba_z', 512), ('swa_z', 512), ('ret_v', 512), ('ret_z', 512), ('nsa_z', 512),
    ('swa_v', 128), ('nsa_vs', 128), ('nsa_vw', 128), ('nsa_g', 128),
    ('nsa_kc', 128), ('nsa_vc', 128), (None, 256),
)
_COL_SCALE = {'moba_q': 0.125 * LOG2E, 'swa_q': 0.125 * LOG2E, 'nsa_q': 0.125 * LOG2E, 'ret_k': 0.125}


def _slab_layout():
    src_off, off = {}, 0
    for name, w in _IN_SPLITS:
        src_off[name] = off
        off += w
    col, pos = {}, 0
    for name, w in _SLAB:
        if name is not None:
            col[name] = pos
        pos += w
    return src_off, col, pos


_SRC_OFF, COL, D_SLAB = _slab_layout()


def _slab_weights(w):
    parts = []
    for name, width in _SLAB:
        if name is None:
            parts.append(jnp.zeros((w.shape[0], width), BF16))
            continue
        blk = w[:, _SRC_OFF[name]:_SRC_OFF[name] + _WIDTH[name]]
        if name in _COL_SCALE:
            blk = blk * _COL_SCALE[name]
        parts.append(jnp.pad(blk.astype(BF16), ((0, 0), (0, width - _WIDTH[name]))))
    return jnp.concatenate(parts, axis=1)
ROPE_COL_LO, ROPE_COL_HI = COL['moba_q'], COL['moba_v']
assert ROPE_COL_LO % PROJ_TN == 0 and ROPE_COL_HI % PROJ_TN == 0 and D_SLAB % PROJ_TN == 0
KCVC_TILE, KCVC_OFF = divmod(COL['nsa_kc'], PROJ_TN)


def _cb(name, width=LANES):
    assert COL[name] % width == 0
    return COL[name] // width


NT = (((1,), (1,)), ((), ()))


def _dot(a, b):
    return jnp.dot(a, b, preferred_element_type=F32)


def _dot_nt(a, b):
    return lax.dot_general(a, b, NT, preferred_element_type=F32)


def _silu(z):
    return z * jax.nn.sigmoid(z)


def _rope_chunk(y, cos, sin_a, sin_b):
    return y * cos + pltpu.roll(y, 96, 1) * sin_a + pltpu.roll(y, 32, 1) * sin_b


def _split3(x):
    x1 = x.astype(BF16).astype(F32)
    x2 = (x - x1).astype(BF16).astype(F32)
    x3 = (x - x1 - x2).astype(BF16).astype(F32)
    return x1, x2, x3


def _proj_kernel(x_ref, g_ref, w_ref, cos_ref, sa_ref, sb_ref, o_ref, kcvc_ref, h_ref):
    j = pl.program_id(1)

    @pl.when(j == 0)
    def _():
        x = x_ref[...]
        ms = jnp.mean(x * x, axis=-1, keepdims=True)
        h_ref[...] = (x * lax.rsqrt(ms + NORM_EPS) * g_ref[...]).astype(BF16)

    y = _dot(h_ref[...], w_ref[...])
    is_rope = jnp.logical_and(j >= ROPE_COL_LO // PROJ_TN, j < ROPE_COL_HI // PROJ_TN)

    @pl.when(j == KCVC_TILE)
    def _():
        for c in range(2 * NSA_KV_HEADS):
            lo = KCVC_OFF + c * HEAD_DIM
            kcvc_ref[c] = y[:, lo:lo + HEAD_DIM].astype(BF16)

    @pl.when(is_rope)
    def _():
        cos, sa, sb = cos_ref[...], sa_ref[...], sb_ref[...]
        for c in range(PROJ_TN // LANES):
            sl = slice(c * LANES, (c + 1) * LANES)
            o_ref[:, sl] = _rope_chunk(y[:, sl], cos, sa, sb).astype(BF16)

    @pl.when(jnp.logical_not(is_rope))
    def _():
        o_ref[...] = y.astype(BF16)


def _proj_in(x2, norm_g, w_slab, cos_t, sa_t, sb_t, seq):
    t = x2.shape[0]
    tm = min(PROJ_TM, seq)
    per_seq = seq // tm
    return pl.pallas_call(
        _proj_kernel,
        grid=(t // tm, D_SLAB // PROJ_TN),
        in_specs=[
            pl.BlockSpec((tm, D_MODEL), lambda i, j: (i, 0)),
            pl.BlockSpec((1, D_MODEL), lambda i, j: (0, 0)),
            pl.BlockSpec((D_MODEL, PROJ_TN), lambda i, j: (0, j)),
            pl.BlockSpec((tm, LANES), lambda i, j: (i % per_seq, 0)),
            pl.BlockSpec((tm, LANES), lambda i, j: (i % per_seq, 0)),
            pl.BlockSpec((tm, LANES), lambda i, j: (i % per_seq, 0)),
        ],
        out_specs=[
            pl.BlockSpec((tm, PROJ_TN), lambda i, j: (i, j)),
            pl.BlockSpec((2 * NSA_KV_HEADS, tm, HEAD_DIM), lambda i, j: (0, i, 0)),
        ],
        out_shape=[
            jax.ShapeDtypeStruct((t, D_SLAB), BF16),
            jax.ShapeDtypeStruct((2 * NSA_KV_HEADS, t, HEAD_DIM), BF16),
        ],
        scratch_shapes=[pltpu.VMEM((tm, D_MODEL), BF16)],
        compiler_params=pltpu.CompilerParams(dimension_semantics=("parallel", "arbitrary")),
        name="proj_in",
    )(x2, norm_g.reshape(1, D_MODEL), w_slab, cos_t, sa_t, sb_t)


def _rank_before(scores, n):
    j_iota = lax.broadcasted_iota(jnp.int32, scores.shape, 0)
    cnt = jnp.zeros(scores.shape, F32)
    for jp in range(n):
        r = scores[jp:jp + 1, :]
        ahead = jnp.logical_or(r > scores, jnp.logical_and(r == scores, jp < j_iota))
        cnt = cnt + jnp.where(ahead, 1.0, 0.0)
    return cnt


def _normalize(o, ones_lane):
    return o * (1.0 / jnp.maximum(o[:, ones_lane:ones_lane + 1], TINY))


def _prefix_attention(qaug, k_at, v_at, n_tiles, ones_lane):
    n_past = (n_tiles - 1) * TQ
    row = lax.broadcasted_iota(jnp.int32, (TQ, TQ), 0)
    col = lax.broadcasted_iota(jnp.int32, (TQ, TQ), 1)
    s_d = jnp.where(col <= row, _dot_nt(qaug, k_at(n_past, TQ)), NEG)
    m = jnp.max(s_d, axis=-1, keepdims=True)
    if n_past:
        s_p = _dot_nt(qaug, k_at(0, n_past))
        m = jnp.maximum(m, jnp.max(s_p, axis=-1, keepdims=True))
    o = _dot(jnp.exp2(s_d - m).astype(BF16), v_at(n_past, TQ))
    if n_past:
        o = o + _dot(jnp.exp2(s_p - m).astype(BF16), v_at(0, n_past))
    return _normalize(o, ones_lane)


def _bias_lanes(allowed_t, row0):
    n, rows = allowed_t.shape
    pieces = []
    if row0:
        pieces.append(jnp.zeros((row0, rows), F32))
    pieces.append(allowed_t)
    if LANES - row0 - n:
        pieces.append(jnp.zeros((LANES - row0 - n, rows), F32))
    full = jnp.concatenate(pieces, axis=0)
    return ((full.T - 1.0) * (-NEG)).astype(BF16)


def _moba_kernel(q_ref, k_ref, v_ref, z_ref, o_ref, kaug_ref, vaug_ref, bias_ref, *, seq):
    t = pl.program_id(1)
    nb = seq // MOBA_BLOCK
    n_pair = MOBA_HEADS // 2

    @pl.when(t == 0)
    def _():
        rowblk = lax.broadcasted_iota(jnp.int32, (seq, LANES), 0) // MOBA_BLOCK
        lane = lax.broadcasted_iota(jnp.int32, (seq, LANES), 1)
        low = lane < HEAD_DIM
        blk = lax.broadcasted_iota(jnp.int32, (nb, seq), 0)
        pos = lax.broadcasted_iota(jnp.int32, (nb, seq), 1) // MOBA_BLOCK
        avg = jnp.where(blk == pos, 1.0 / MOBA_BLOCK, 0.0).astype(BF16)
        for p in range(n_pair):
            sl = slice(p * LANES, (p + 1) * LANES)
            kf = k_ref[:, sl].astype(F32)
            vf = v_ref[:, sl].astype(F32)
            kaug_ref[2 * p] = jnp.where(low, kf, jnp.where(lane - HEAD_DIM == rowblk, 1.0, 0.0)).astype(BF16)
            kaug_ref[2 * p + 1] = jnp.where(low, jnp.where(lane == rowblk, 1.0, 0.0), kf).astype(BF16)
            vaug_ref[2 * p] = jnp.where(low, vf, 1.0).astype(BF16)
            vaug_ref[2 * p + 1] = jnp.where(low, 1.0, vf).astype(BF16)

            km3 = jnp.concatenate(_split3(_dot(avg, k_ref[:, sl])), axis=0).astype(BF16)
            q_all = q_ref[:, sl]
            lane_q = lax.broadcasted_iota(jnp.int32, (seq, LANES), 1)
            j_iota = lax.broadcasted_iota(jnp.int32, (nb, seq), 0)
            own = lax.broadcasted_iota(jnp.int32, (nb, seq), 1) // MOBA_BLOCK
            for h in range(2):
                mine = (lane_q < HEAD_DIM) if h == 0 else (lane_q >= HEAD_DIM)
                g3 = _dot_nt(km3, jnp.where(mine, q_all, jnp.zeros((), BF16)))
                gate = g3[:nb] + g3[nb:2 * nb] + g3[2 * nb:]
                gate = jnp.where(j_iota < own, gate, -jnp.inf)
                cnt = _rank_before(gate, nb)
                allowed = jnp.logical_or(jnp.logical_and(cnt < MOBA_TOPK, j_iota < own), j_iota == own)
                bias_ref[2 * p + h] = _bias_lanes(jnp.where(allowed, 1.0, 0.0), HEAD_DIM if h == 0 else 0)

    rows = pl.ds(pl.multiple_of(t * TQ, TQ), TQ)
    lane = lax.broadcasted_iota(jnp.int32, (TQ, LANES), 1)
    qaugs = []
    for p in range(n_pair):
        q = q_ref[rows, p * LANES:(p + 1) * LANES]
        for h in range(2):
            mine = (lane < HEAD_DIM) if h == 0 else (lane >= HEAD_DIM)
            qaugs.append(jnp.where(mine, q, bias_ref[2 * p + h, rows, :]))
    gate_z = _silu(z_ref[...].astype(F32))

    for tt in range(nb):
        @pl.when(t == tt)
        def _(tt=tt):
            for p in range(n_pair):
                outs = [
                    _prefix_attention(
                        qaugs[2 * p + h],
                        lambda start, size, i=2 * p + h: kaug_ref[i, start:start + size, :],
                        lambda start, size, i=2 * p + h: vaug_ref[i, start:start + size, :],
                        tt + 1, HEAD_DIM if h == 0 else 0)
                    for h in range(2)
                ]
                sl = slice(p * LANES, (p + 1) * LANES)
                o_ref[:, sl] = (jnp.where(lane < HEAD_DIM, outs[0], outs[1]) * gate_z[:, sl]).astype(BF16)


def _moba(slab, batch, seq):
    nt = seq // TQ
    w = MOBA_HEADS * HEAD_DIM
    nb = seq // MOBA_BLOCK
    return pl.pallas_call(
        functools.partial(_moba_kernel, seq=seq),
        grid=(batch, nt),
        in_specs=[
            pl.BlockSpec((seq, w), lambda b, t: (b, _cb('moba_q', w))),
            pl.BlockSpec((seq, w), lambda b, t: (b, _cb('moba_k', w))),
            pl.BlockSpec((seq, w), lambda b, t: (b, _cb('moba_v', w))),
            pl.BlockSpec((TQ, w), lambda b, t: (b * nt + t, _cb('moba_z', w))),
        ],
        out_specs=pl.BlockSpec((TQ, w), lambda b, t: (b * nt + t, 0)),
        out_shape=jax.ShapeDtypeStruct((batch * seq, w), BF16),
        scratch_shapes=[
            pltpu.VMEM((MOBA_HEADS, seq, LANES), BF16),
            pltpu.VMEM((MOBA_HEADS, seq, LANES), BF16),
            pltpu.VMEM((MOBA_HEADS, seq, LANES), BF16),
        ],
        compiler_params=pltpu.CompilerParams(dimension_semantics=("parallel", "arbitrary")),
        name="moba",
    )(slab, slab, slab, slab)


GQA_W = 4 * HEAD_DIM
ONES_A, ONES_B = 3 * HEAD_DIM, 0


def _rep2(x, hk):
    lane = lax.broadcasted_iota(jnp.int32, x.shape, 1)
    keep = (lane < HEAD_DIM) == (hk == 0)
    return jnp.where(keep, x, pltpu.roll(x, HEAD_DIM, 1))


def _value_variants(v2):
    low = lax.broadcasted_iota(jnp.int32, v2.shape, 1) < HEAD_DIM
    va = jnp.concatenate([v2, jnp.where(low, v2, 1.0)], axis=1).astype(BF16)
    vb = jnp.concatenate([jnp.where(low, 1.0, v2), v2], axis=1).astype(BF16)
    return va, vb


def _head_mask(g, width, rows=TQ):
    lane = lax.broadcasted_iota(jnp.int32, (rows, width), 1)
    return (lane // HEAD_DIM) == g


SWA_SPAN = TQ + LANES


def _swa_kernel(sink_ref, q_ref, k_ref, v_ref, z_ref, o_ref, krep_ref, va_ref, vb_ref):
    hk = pl.program_id(1)
    t = pl.program_id(2)

    @pl.when(t == 0)
    def _():
        kr = _rep2(k_ref[...].astype(F32), hk).astype(BF16)
        krep_ref[...] = jnp.concatenate([kr, kr], axis=1)
        va_ref[...], vb_ref[...] = _value_variants(_rep2(v_ref[...].astype(F32), hk))

    q = q_ref[...]
    k0 = pl.multiple_of(jnp.maximum(t * TQ - LANES, 0), LANES)
    kw = krep_ref[pl.ds(k0, SWA_SPAN), :]
    dist = (t * TQ + lax.broadcasted_iota(jnp.int32, (TQ, SWA_SPAN), 0)) - (
        k0 + lax.broadcasted_iota(jnp.int32, (TQ, SWA_SPAN), 1))
    mask = jnp.logical_and(dist >= 0, dist < SWA_WINDOW)
    out = jnp.zeros((TQ, GQA_W), F32)
    for g in range(4):
        mine = _head_mask(g, GQA_W)
        vref, ones = (va_ref, ONES_A) if g < 3 else (vb_ref, ONES_B)
        s = jnp.where(mask, _dot_nt(jnp.where(mine, q, jnp.zeros((), BF16)), kw), NEG)
        sink = sink_ref[hk * 4 + g] * LOG2E
        m = jnp.maximum(jnp.max(s, axis=-1, keepdims=True), sink)
        o = _dot(jnp.exp2(s - m).astype(BF16), vref[pl.ds(k0, SWA_SPAN), :])
        den = o[:, ones:ones + 1] + jnp.exp2(sink - m)
        out = jnp.where(mine, o * (1.0 / jnp.maximum(den, TINY)), out)
    o_ref[...] = (out * _silu(z_ref[...].astype(F32))).astype(BF16)


def _swa(slab, sink, batch, seq):
    nt = seq // TQ
    w = GQA_W
    return pl.pallas_call(
        _swa_kernel,
        grid=(batch, SWA_KV_HEADS, nt),
        in_specs=[
            pl.BlockSpec(memory_space=pltpu.SMEM),
            pl.BlockSpec((TQ, w), lambda b, h, t: (b * nt + t, _cb('swa_q', w) + h)),
            pl.BlockSpec((seq, LANES), lambda b, h, t: (b, _cb('swa_k'))),
            pl.BlockSpec((seq, LANES), lambda b, h, t: (b, _cb('swa_v'))),
            pl.BlockSpec((TQ, w), lambda b, h, t: (b * nt + t, _cb('swa_z', w) + h)),
        ],
        out_specs=pl.BlockSpec((TQ, w), lambda b, h, t: (b * nt + t, h)),
        out_shape=jax.ShapeDtypeStruct((batch * seq, SWA_HEADS * HEAD_DIM), BF16),
        scratch_shapes=[pltpu.VMEM((seq, w), BF16), pltpu.VMEM((seq, w), BF16), pltpu.VMEM((seq, w), BF16)],
        compiler_params=pltpu.CompilerParams(dimension_semantics=("parallel", "parallel", "arbitrary")),
        name="swa",
    )(sink, slab, slab, slab, slab)


def _ret_kernel(qk_ref, v_ref, z_ref, o_ref, state_ref):
    c = RET_CHUNK

    @pl.when(pl.program_id(1) == 0)
    def _():
        state_ref[...] = jnp.zeros(state_ref.shape, F32)

    q = qk_ref[:, :RET_HEADS * RET_QK_DIM]
    k = qk_ref[:, RET_HEADS * RET_QK_DIM:]
    ii = lax.broadcasted_iota(jnp.int32, (c, c), 0)
    jj = lax.broadcasted_iota(jnp.int32, (c, c), 1)
    diff = (ii - jj).astype(F32)
    row = lax.broadcasted_iota(jnp.int32, (c, 1), 0).astype(F32)
    lane = lax.broadcasted_iota(jnp.int32, (c, RET_HEADS * RET_QK_DIM), 1)
    for h in range(RET_HEADS):
        log_g = math.log(1.0 - 2.0 ** (-5.0 - h))
        intra = jnp.where(diff >= 0, jnp.exp(jnp.maximum(diff, 0.0) * log_g), 0.0)
        q_dec = jnp.exp((row + 1.0) * log_g)
        k_dec = jnp.exp((c - 1.0 - row) * log_g)
        chunk_dec = math.exp(c * log_g)
        qm = jnp.where((lane // RET_QK_DIM) == h, q, jnp.zeros((), BF16))
        vh = v_ref[:, h * RET_V_DIM:(h + 1) * RET_V_DIM]
        att = _dot_nt(qm, k) * intra
        st = state_ref[h]
        o = _dot(att.astype(BF16), vh) + _dot(qm, st.astype(BF16)) * q_dec
        kd = (k.astype(F32) * k_dec).T.astype(BF16)
        state_ref[h] = st * chunk_dec + _dot(kd, vh)
        mu = jnp.mean(o, axis=-1, keepdims=True)
        var = jnp.mean(jnp.square(o - mu), axis=-1, keepdims=True)
        o = (o - mu) * lax.rsqrt(var + NORM_EPS)
        zh = z_ref[:, h * RET_V_DIM:(h + 1) * RET_V_DIM].astype(F32)
        o_ref[:, h * RET_V_DIM:(h + 1) * RET_V_DIM] = (o * _silu(zh)).astype(BF16)


def _retention(slab, batch, seq):
    nc = seq // RET_CHUNK
    w = RET_HEADS * RET_V_DIM
    return pl.pallas_call(
        _ret_kernel,
        grid=(batch, nc),
        in_specs=[
            pl.BlockSpec((RET_CHUNK, w), lambda b, c: (b * nc + c, _cb('ret_q', w))),
            pl.BlockSpec((RET_CHUNK, w), lambda b, c: (b * nc + c, _cb('ret_v', w))),
            pl.BlockSpec((RET_CHUNK, w), lambda b, c: (b * nc + c, _cb('ret_z', w))),
        ],
        out_specs=pl.BlockSpec((RET_CHUNK, w), lambda b, c: (b * nc + c, 0)),
        out_shape=jax.ShapeDtypeStruct((batch * seq, w), BF16),
        scratch_shapes=[pltpu.VMEM((RET_HEADS, RET_HEADS * RET_QK_DIM, RET_V_DIM), F32)],
        compiler_params=pltpu.CompilerParams(dimension_semantics=("parallel", "arbitrary")),
        name="retention",
    )(slab, slab, slab)


def _compress_kernel(r_ref, w1_ref, w2_ref, pos_ref, cos_ref, sa_ref, sb_ref, o_ref):
    kind = pl.program_id(0) // NSA_KV_HEADS
    half = NSA_CMP_STRIDE * HEAD_DIM
    r = r_ref[...]
    n_rows = r.shape[0]
    lo = _dot(r, w1_ref[:half, :])
    hi = _dot(r, w1_ref[half:, :])
    pos = _dot(pos_ref[...], w1_ref[...])[0:1, :]
    pre = lo + pltpu.roll(hi, n_rows - 1, 0) + pos
    y = _dot(jax.nn.gelu(pre).astype(BF16), w2_ref[...])
    roped = _rope_chunk(y, cos_ref[...], sa_ref[...], sb_ref[...])
    o_ref[...] = jnp.where(kind == 0, roped, y)


def _compress(r, w1, w2, pos8, cos_c, sa_c, sb_c, batch):
    n_rows = r.shape[2]
    flat = NSA_CMP_LEN * HEAD_DIM
    return pl.pallas_call(
        _compress_kernel,
        grid=(2 * NSA_KV_HEADS, batch),
        in_specs=[
            pl.BlockSpec((None, None, n_rows, flat // 2), lambda i, b: (i, b, 0, 0)),
            pl.BlockSpec((None, flat, NSA_CMP_HIDDEN), lambda i, b: (i // NSA_KV_HEADS, 0, 0)),
            pl.BlockSpec((None, NSA_CMP_HIDDEN, LANES), lambda i, b: (i // NSA_KV_HEADS, 0, 0)),
            pl.BlockSpec((8, flat), lambda i, b: (0, 0)),
            pl.BlockSpec((n_rows, LANES), lambda i, b: (0, 0)),
            pl.BlockSpec((n_rows, LANES), lambda i, b: (0, 0)),
            pl.BlockSpec((n_rows, LANES), lambda i, b: (0, 0)),
        ],
        out_specs=pl.BlockSpec((None, None, n_rows, LANES), lambda i, b: (i, b, 0, 0)),
        out_shape=jax.ShapeDtypeStruct((2 * NSA_KV_HEADS, batch, n_rows, LANES), F32),
        compiler_params=pltpu.CompilerParams(dimension_semantics=("parallel", "parallel")),
        name="nsa_compress",
    )(r, w1, w2, pos8, cos_c, sa_c, sb_c)


NSA_WIN_SPAN = TQ + NSA_WINDOW
SEL_LANE0 = 64


def _nsa_kernel(q_ref, ks_ref, vs_ref, kw_ref, vw_ref, kc_ref, vc_ref, g_ref, z_ref, o_ref,
                ksa_ref, ksb_ref, vsa_ref, vsb_ref, kwr_ref, vwa_ref, vwb_ref, kcr_ref, vcr_ref,
                ocmp_ref, bias_ref, *, seq):
    hk = pl.program_id(1)
    t = pl.program_id(2)
    n_slc = seq // NSA_SLC_BLOCK
    n_cmp = (seq - NSA_CMP_LEN) // NSA_CMP_STRIDE + 1
    n_crow = kc_ref.shape[0]
    w = GQA_W

    @pl.when(t == 0)
    def _():
        ks = _rep2(ks_ref[...].astype(F32), hk).astype(BF16)
        rowblk = lax.broadcasted_iota(jnp.int32, (seq, LANES), 0) // NSA_SLC_BLOCK
        lane = lax.broadcasted_iota(jnp.int32, (seq, LANES), 1)
        onehot = jnp.where(lane - SEL_LANE0 == rowblk, 1.0, 0.0).astype(BF16)
        ksa_ref[...] = jnp.concatenate([ks, onehot], axis=1)
        ksb_ref[...] = jnp.concatenate([onehot, ks], axis=1)
        vsa_ref[...], vsb_ref[...] = _value_variants(_rep2(vs_ref[...].astype(F32), hk))
        kwin = _rep2(kw_ref[...].astype(F32), hk).astype(BF16)
        kwr_ref[...] = jnp.concatenate([kwin, kwin], axis=1)
        vwa_ref[...], vwb_ref[...] = _value_variants(_rep2(vw_ref[...].astype(F32), hk))
        kc = kc_ref[...]
        kc = (kc + pltpu.roll(kc, HEAD_DIM, 1)).astype(BF16)
        kcr_ref[...] = jnp.concatenate([kc, kc], axis=1)
        vc = vc_ref[...]
        vc = (vc + pltpu.roll(vc, HEAD_DIM, 1)).astype(BF16)
        vcr_ref[...] = jnp.concatenate([vc, vc], axis=1)

        q_all = q_ref[...]
        pos_all = lax.broadcasted_iota(jnp.int32, (seq, 1), 0)
        ci = lax.broadcasted_iota(jnp.int32, (1, n_crow), 1)
        cmask = jnp.logical_and(ci * NSA_CMP_STRIDE + NSA_CMP_LEN - 1 <= pos_all, ci < n_cmp)
        kcr, vcr = kcr_ref[...], vcr_ref[...]
        p_sum = jnp.zeros((seq, n_crow), F32)
        o_cmp = jnp.zeros((seq, w), F32)
        for g in range(4):
            mask_g = _head_mask(g, w, seq)
            s = jnp.where(cmask, _dot_nt(jnp.where(mask_g, q_all, jnp.zeros((), BF16)), kcr), NEG)
            m = jnp.max(s, axis=-1, keepdims=True)
            e = jnp.where(cmask, jnp.exp2(s - m), 0.0)
            p = e * (1.0 / jnp.maximum(jnp.sum(e, axis=-1, keepdims=True), TINY))
            p_sum = p_sum + p
            o_cmp = jnp.where(mask_g, _dot(p.astype(BF16), vcr), o_cmp)
        ocmp_ref[...] = o_cmp

        c0 = lax.broadcasted_iota(jnp.int32, (n_slc, n_crow), 1) * NSA_CMP_STRIDE
        s0 = lax.broadcasted_iota(jnp.int32, (n_slc, n_crow), 0) * NSA_SLC_BLOCK
        ov = jnp.minimum(c0 + NSA_CMP_LEN, s0 + NSA_SLC_BLOCK) - jnp.maximum(c0, s0)
        ov = (jnp.maximum(ov, 0).astype(F32) * (1.0 / NSA_CMP_LEN)).astype(BF16)
        imp = _dot_nt(jnp.concatenate([ov, ov, ov], axis=1),
                      jnp.concatenate(_split3(p_sum), axis=1).astype(BF16))
        j_iota = lax.broadcasted_iota(jnp.int32, (n_slc, seq), 0)
        q_blk = lax.broadcasted_iota(jnp.int32, (n_slc, seq), 1) // NSA_SLC_BLOCK
        forced = jnp.logical_or(j_iota == 0, jnp.logical_or(j_iota == q_blk, j_iota == q_blk - 1))
        imp = jnp.where(forced, jnp.inf, imp)
        imp = jnp.where(j_iota <= q_blk, imp, -jnp.inf)
        cnt = _rank_before(imp, n_slc)
        allowed = jnp.logical_and(cnt < min(NSA_SLC_TOPN, n_slc), j_iota <= q_blk)
        bias_ref[...] = _bias_lanes(jnp.where(allowed, 1.0, 0.0), SEL_LANE0)

    rows = pl.ds(pl.multiple_of(t * TQ, TQ), TQ)
    q = q_ref[rows, :]
    bias = bias_ref[rows, :]
    zero = jnp.zeros((), BF16)
    masks = [_head_mask(g, w) for g in range(4)]
    qms = [jnp.where(masks[g], q, zero) for g in range(4)]
    qaugs = []
    for g in range(4):
        half_mask = _head_mask(g % 2, LANES)
        if g < 2:
            qaugs.append(jnp.concatenate([jnp.where(half_mask, q[:, :LANES], zero), bias], axis=1))
        else:
            qaugs.append(jnp.concatenate([bias, jnp.where(half_mask, q[:, LANES:], zero)], axis=1))

    gates = jax.nn.sigmoid(g_ref[...].astype(F32))
    g_hi = gates.astype(BF16)
    g_lo = (gates - g_hi.astype(F32)).astype(BF16)
    er = lax.broadcasted_iota(jnp.int32, (2 * LANES, 3 * w), 0) % LANES
    ec = lax.broadcasted_iota(jnp.int32, (2 * LANES, 3 * w), 1)
    expand = jnp.where(er == (ec // w) * NSA_HEADS + hk * 4 + (ec % w) // HEAD_DIM, 1.0, 0.0).astype(BF16)
    gx = _dot(jnp.concatenate([g_hi, g_lo], axis=1), expand)
    o_cmp = gx[:, :w] * ocmp_ref[rows, :]
    gate_z = _silu(z_ref[...].astype(F32))

    for tt in range(seq // TQ):
        @pl.when(t == tt)
        def _(tt=tt):
            k0 = max(tt * TQ - NSA_WINDOW, 0)
            kwin = kwr_ref[k0:k0 + NSA_WIN_SPAN, :]
            dist = (tt * TQ + lax.broadcasted_iota(jnp.int32, (TQ, 1), 0)) - (
                k0 + lax.broadcasted_iota(jnp.int32, (1, NSA_WIN_SPAN), 1))
            wmask = jnp.logical_and(dist >= 0, dist < NSA_WINDOW)
            o_win = jnp.zeros((TQ, w), F32)
            o_slc = jnp.zeros((TQ, w), F32)
            for g in range(4):
                vref, ones = (vwa_ref, ONES_A) if g < 3 else (vwb_ref, ONES_B)
                s = jnp.where(wmask, _dot_nt(qms[g], kwin), NEG)
                m = jnp.max(s, axis=-1, keepdims=True)
                o = _dot(jnp.exp2(s - m).astype(BF16), vref[k0:k0 + NSA_WIN_SPAN, :])
                o_win = jnp.where(masks[g], _normalize(o, ones), o_win)

                kref = ksa_ref if g < 2 else ksb_ref
                vref, ones = (vsa_ref, ONES_A) if g < 3 else (vsb_ref, ONES_B)
                o = _prefix_attention(
                    qaugs[g],
                    lambda start, size, kref=kref: kref[start:start + size, :],
                    lambda start, size, vref=vref: vref[start:start + size, :],
                    tt + 1, ones)
                o_slc = jnp.where(masks[g], o, o_slc)
            o = o_cmp + gx[:, w:2 * w] * o_slc + gx[:, 2 * w:] * o_win
            o_ref[...] = (o * gate_z).astype(BF16)


def _nsa(slab, cmp_kv, batch, seq):
    nt = seq // TQ
    w = GQA_W
    n_crow = cmp_kv.shape[2]
    kv_spec = lambda name: pl.BlockSpec((seq, LANES), lambda b, h, t: (b, _cb(name)))
    big = pltpu.VMEM((seq, w), BF16)
    return pl.pallas_call(
        functools.partial(_nsa_kernel, seq=seq),
        grid=(batch, NSA_KV_HEADS, nt),
        in_specs=[
            pl.BlockSpec((seq, w), lambda b, h, t: (b, _cb('nsa_q', w) + h)),
            kv_spec('nsa_ks'), kv_spec('nsa_vs'), kv_spec('nsa_kw'), kv_spec('nsa_vw'),
            pl.BlockSpec((None, None, n_crow, LANES), lambda b, h, t: (h, b, 0, 0)),
            pl.BlockSpec((None, None, n_crow, LANES), lambda b, h, t: (NSA_KV_HEADS + h, b, 0, 0)),
            pl.BlockSpec((TQ, LANES), lambda b, h, t: (b * nt + t, _cb('nsa_g'))),
            pl.BlockSpec((TQ, w), lambda b, h, t: (b * nt + t, _cb('nsa_z', w) + h)),
        ],
        out_specs=pl.BlockSpec((TQ, w), lambda b, h, t: (b * nt + t, h)),
        out_shape=jax.ShapeDtypeStruct((batch * seq, NSA_HEADS * HEAD_DIM), BF16),
        scratch_shapes=[
            big, big, big, big, big, big, big,
            pltpu.VMEM((n_crow, w), BF16), pltpu.VMEM((n_crow, w), BF16),
            pltpu.VMEM((seq, w), F32), pltpu.VMEM((seq, LANES), BF16),
        ],
        compiler_params=pltpu.CompilerParams(dimension_semantics=("parallel", "parallel", "arbitrary")),
        name="nsa",
    )(slab, slab, slab, slab, slab, cmp_kv, cmp_kv, slab, slab)


def _merge_kernel(x_ref, g_ref, oa_ref, ob_ref, oc_ref, od_ref, wb_ref, wo_ref, fg_ref, o_ref, *, final):
    merged = None
    row = 0
    for i, br in enumerate((oa_ref, ob_ref, oc_ref, od_ref)):
        width = br.shape[1]
        y = _dot(br[...], wb_ref[row:row + width, :])
        gate = jax.nn.sigmoid(g_ref[:, i * D_MODEL:(i + 1) * D_MODEL].astype(F32))
        merged = gate * y if merged is None else merged + gate * y
        row += width
    x = x_ref[...] + _dot(merged.astype(BF16), wo_ref[...])
    if final:
        ms = jnp.mean(x * x, axis=-1, keepdims=True)
        x = x * lax.rsqrt(ms + NORM_EPS) * fg_ref[...]
    o_ref[...] = x


def _merge(x2, slab, outs, w_branch, w_out, final_g, final):
    t = x2.shape[0]
    tm = MERGE_TM
    d_branch = w_branch.shape[0]
    row_spec = lambda width: pl.BlockSpec((tm, width), lambda i: (i, 0))
    return pl.pallas_call(
        functools.partial(_merge_kernel, final=final),
        grid=(t // tm,),
        in_specs=[
            row_spec(D_MODEL),
            pl.BlockSpec((tm, N_BRANCH * D_MODEL), lambda i: (i, _cb('merge_g', N_BRANCH * D_MODEL))),
            row_spec(outs[0].shape[1]), row_spec(outs[1].shape[1]),
            row_spec(outs[2].shape[1]), row_spec(outs[3].shape[1]),
            pl.BlockSpec((d_branch, D_MODEL), lambda i: (0, 0)),
            pl.BlockSpec((D_MODEL, D_MODEL), lambda i: (0, 0)),
            pl.BlockSpec((1, D_MODEL), lambda i: (0, 0)),
        ],
        out_specs=row_spec(D_MODEL),
        out_shape=jax.ShapeDtypeStruct((t, D_MODEL), F32),
        compiler_params=pltpu.CompilerParams(dimension_semantics=("parallel",)),
        name="merge_out",
    )(x2, slab, *outs, w_branch, w_out, final_g.reshape(1, D_MODEL))


def _rope_tables(positions):
    inv = 1.0 / (ROPE_THETA ** (jnp.arange(0, HEAD_DIM, 2, dtype=F32) / HEAD_DIM))
    ang = positions.astype(F32)[:, None] * inv[None, :]
    cos, sin = jnp.cos(ang), jnp.sin(ang)
    zero = jnp.zeros_like(sin)
    cos_t = jnp.tile(cos, (1, 4))
    sa_t = jnp.tile(jnp.concatenate([-sin, zero], axis=1), (1, 2))
    sb_t = jnp.tile(jnp.concatenate([zero, sin], axis=1), (1, 2))
    return cos_t, sa_t, sb_t


def kernel(x, norm_g, w_in, w_branch, w_out, swa_sink, nsa_cmp_pos, nsa_w_ck1, nsa_w_ck2,
           nsa_w_cv1, nsa_w_cv2, final_norm_g):
    batch, seq, d = x.shape
    depth = w_in.shape[0]
    assert d == D_MODEL and seq % TQ == 0 and seq % RET_CHUNK == 0 and (batch * seq) % MERGE_TM == 0
    assert seq >= NSA_WIN_SPAN and seq % (8 * NSA_CMP_STRIDE) == 0 and (seq // MOBA_BLOCK) % 8 == 0

    cos_t, sa_t, sb_t = _rope_tables(jnp.arange(seq))
    n_crow = seq // NSA_CMP_STRIDE
    cos_c, sa_c, sb_c = _rope_tables(jnp.arange(n_crow) * NSA_CMP_STRIDE + NSA_CMP_LEN - 1)

    x2 = x.reshape(batch * seq, d)
    for l in range(depth):
        slab, kcvc = _proj_in(x2, norm_g[l], _slab_weights(w_in[l]), cos_t, sa_t, sb_t, seq)

        groups = kcvc.reshape(2 * NSA_KV_HEADS, batch, n_crow, NSA_CMP_STRIDE * HEAD_DIM)
        w1 = jnp.stack([nsa_w_ck1[l], nsa_w_cv1[l]]).astype(BF16)
        w2 = jnp.pad(jnp.stack([nsa_w_ck2[l], nsa_w_cv2[l]]), ((0, 0), (0, 0), (0, LANES - HEAD_DIM))).astype(BF16)
        pos8 = jnp.broadcast_to(nsa_cmp_pos[l].reshape(1, -1), (8, NSA_CMP_LEN * HEAD_DIM)).astype(BF16)
        cmp_kv = _compress(groups, w1, w2, pos8, cos_c, sa_c, sb_c, batch)

        outs = (
            _moba(slab, batch, seq),
            _swa(slab, swa_sink[l], batch, seq),
            _retention(slab, batch, seq),
            _nsa(slab, cmp_kv, batch, seq),
        )
        x2 = _merge(x2, slab, outs, w_branch[l].astype(BF16), w_out[l].astype(BF16),
                    final_norm_g, final=(l == depth - 1))
    return x2.reshape(batch, seq, d)
```

```python
import functools
import math

import numpy as np
import jax
import jax.numpy as jnp
from jax import lax
from jax.experimental import pallas as pl
from jax.experimental.pallas import tpu as pltpu

F32 = jnp.float32
BF16 = jnp.bfloat16

D_MODEL = 1024
HEAD_DIM = 64
ROPE_THETA = 10000.0
NORM_EPS = 1e-6
TINY = 1e-30
N_BRANCH = 4
NEG = -1e30
LOG2E = 1.4426950408889634

MOBA_HEADS = 8
MOBA_BLOCK = 256
MOBA_TOPK = 3

SWA_HEADS = 8
SWA_KV_HEADS = 2
SWA_WINDOW = 128

RET_HEADS = 4
RET_QK_DIM = 64
RET_V_DIM = 128

NSA_HEADS = 8
NSA_KV_HEADS = 2
NSA_CMP_LEN = 32
NSA_CMP_STRIDE = 16
NSA_CMP_HIDDEN = 256
NSA_SLC_BLOCK = 64
NSA_SLC_TOPN = 16
NSA_WINDOW = 512

LANES = 128
TQ = 256
MOBA_PAIRS_PER_ITER = 2
PROJ_TM = 2048
PROJ_TN = 1024
MERGE_TM = 512
RET_CHUNK = 256

_IN_SPLITS = (
    ('moba_q', 512), ('moba_k', 512), ('moba_v', 512), ('moba_z', 512),
    ('swa_q', 512), ('swa_k', 128), ('swa_v', 128), ('swa_z', 512),
    ('ret_q', 256), ('ret_k', 256), ('ret_v', 512), ('ret_z', 512),
    ('nsa_q', 512), ('nsa_kc', 128), ('nsa_vc', 128),
    ('nsa_ks', 128), ('nsa_vs', 128), ('nsa_kw', 128), ('nsa_vw', 128),
    ('nsa_g', 24), ('nsa_z', 512),
    ('merge_g', 4096),
)
_WIDTH = dict(_IN_SPLITS)
_SLAB = (
    ('merge_g', 4096),
    ('moba_q', 512), ('moba_k', 512), ('swa_q', 512), ('nsa_q', 512),
    ('ret_q', 256), ('ret_k', 256),
    ('swa_k', 128), ('nsa_ks', 128), ('nsa_kw', 128), (None, 128),
    ('moba_v', 512), ('moba_z', 512), ('swa_z', 512), ('ret_v', 512), ('ret_z', 512), ('nsa_z', 512),
    ('swa_v', 128), ('nsa_vs', 128), ('nsa_vw', 128), ('nsa_g', 128),
    ('nsa_kc', 128), ('nsa_vc', 128), (None, 256),
)
_COL_SCALE = {'moba_q': 0.125 * LOG2E, 'swa_q': 0.125 * LOG2E, 'nsa_q': 0.125 * LOG2E, 'ret_k': 0.125}


def _slab_layout():
    src_off, off = {}, 0
    for name, w in _IN_SPLITS:
        src_off[name] = off
        off += w
    col, pos = {}, 0
    for name, w in _SLAB:
        if name is not None:
            col[name] = pos
        pos += w
    return src_off, col, pos


_SRC_OFF, COL, D_SLAB = _slab_layout()


def _slab_weights(w):
    parts = []
    for name, width in _SLAB:
        if name is None:
            parts.append(jnp.zeros((w.shape[0], width), BF16))
            continue
        blk = w[:, _SRC_OFF[name]:_SRC_OFF[name] + _WIDTH[name]]
        if name in _COL_SCALE:
            blk = blk * _COL_SCALE[name]
        parts.append(jnp.pad(blk.astype(BF16), ((0, 0), (0, width - _WIDTH[name]))))
    return jnp.concatenate(parts, axis=1)
ROPE_COL_LO, ROPE_COL_HI = COL['moba_q'], COL['moba_v']
assert ROPE_COL_LO % PROJ_TN == 0 and ROPE_COL_HI % PROJ_TN == 0 and D_SLAB % PROJ_TN == 0
KCVC_TILE, KCVC_OFF = divmod(COL['nsa_kc'], PROJ_TN)


def _cb(name, width=LANES):
    assert COL[name] % width == 0
    return COL[name] // width


NT = (((1,), (1,)), ((), ()))


def _dot(a, b):
    return jnp.dot(a, b, preferred_element_type=F32)


def _dot_nt(a, b):
    return lax.dot_general(a, b, NT, preferred_element_type=F32)


def _silu(z):
    return z * jax.nn.sigmoid(z)


def _rope_chunk(y, cos, sin_a, sin_b):
    return y * cos + pltpu.roll(y, 96, 1) * sin_a + pltpu.roll(y, 32, 1) * sin_b


def _split3(x):
    x1 = x.astype(BF16).astype(F32)
    x2 = (x - x1).astype(BF16).astype(F32)
    x3 = (x - x1 - x2).astype(BF16).astype(F32)
    return x1, x2, x3


def _proj_kernel(x_ref, g_ref, w_ref, cos_ref, sa_ref, sb_ref, o_ref, kcvc_ref, h_ref):
    j = pl.program_id(1)

    @pl.when(j == 0)
    def _():
        x = x_ref[...]
        ms = jnp.mean(x * x, axis=-1, keepdims=True)
        h_ref[...] = (x * lax.rsqrt(ms + NORM_EPS) * g_ref[...]).astype(BF16)

    y = _dot(h_ref[...], w_ref[...])
    is_rope = jnp.logical_and(j >= ROPE_COL_LO // PROJ_TN, j < ROPE_COL_HI // PROJ_TN)

    @pl.when(j == KCVC_TILE)
    def _():
        for c in range(2 * NSA_KV_HEADS):
            lo = KCVC_OFF + c * HEAD_DIM
            kcvc_ref[c] = y[:, lo:lo + HEAD_DIM].astype(BF16)

    @pl.when(is_rope)
    def _():
        cos, sa, sb = cos_ref[...], sa_ref[...], sb_ref[...]
        for c in range(PROJ_TN // LANES):
            sl = slice(c * LANES, (c + 1) * LANES)
            o_ref[:, sl] = _rope_chunk(y[:, sl], cos, sa, sb).astype(BF16)

    @pl.when(jnp.logical_not(is_rope))
    def _():
        o_ref[...] = y.astype(BF16)


def _proj_in(x2, norm_g, w_slab, cos_t, sa_t, sb_t, seq):
    t = x2.shape[0]
    tm = min(PROJ_TM, seq)
    per_seq = seq // tm
    return pl.pallas_call(
        _proj_kernel,
        grid=(t // tm, D_SLAB // PROJ_TN),
        in_specs=[
            pl.BlockSpec((tm, D_MODEL), lambda i, j: (i, 0)),
            pl.BlockSpec((1, D_MODEL), lambda i, j: (0, 0)),
            pl.BlockSpec((D_MODEL, PROJ_TN), lambda i, j: (0, j)),
            pl.BlockSpec((tm, LANES), lambda i, j: (i % per_seq, 0)),
            pl.BlockSpec((tm, LANES), lambda i, j: (i % per_seq, 0)),
            pl.BlockSpec((tm, LANES), lambda i, j: (i % per_seq, 0)),
        ],
        out_specs=[
            pl.BlockSpec((tm, PROJ_TN), lambda i, j: (i, j)),
            pl.BlockSpec((2 * NSA_KV_HEADS, tm, HEAD_DIM), lambda i, j: (0, i, 0)),
        ],
        out_shape=[
            jax.ShapeDtypeStruct((t, D_SLAB), BF16),
            jax.ShapeDtypeStruct((2 * NSA_KV_HEADS, t, HEAD_DIM), BF16),
        ],
        scratch_shapes=[pltpu.VMEM((tm, D_MODEL), BF16)],
        compiler_params=pltpu.CompilerParams(dimension_semantics=("parallel", "arbitrary")),
        name="proj_in",
    )(x2, norm_g.reshape(1, D_MODEL), w_slab, cos_t, sa_t, sb_t)


def _rank_before(scores, n):
    j_iota = lax.broadcasted_iota(jnp.int32, scores.shape, 0)
    cnt = jnp.zeros(scores.shape, F32)
    for jp in range(n):
        r = scores[jp:jp + 1, :]
        ahead = jnp.logical_or(r > scores, jnp.logical_and(r == scores, jp < j_iota))
        cnt = cnt + jnp.where(ahead, 1.0, 0.0)
    return cnt


def _normalize(o, ones_lane):
    return o * (1.0 / jnp.maximum(o[:, ones_lane:ones_lane + 1], TINY))


def _prefix_attention(qaug, k_at, v_at, n_tiles, ones_lane):
    n_past = (n_tiles - 1) * TQ
    row = lax.broadcasted_iota(jnp.int32, (TQ, TQ), 0)
    col = lax.broadcasted_iota(jnp.int32, (TQ, TQ), 1)
    s_d = jnp.where(col <= row, _dot_nt(qaug, k_at(n_past, TQ)), NEG)
    m = jnp.max(s_d, axis=-1, keepdims=True)
    if n_past:
        s_p = _dot_nt(qaug, k_at(0, n_past))
        m = jnp.maximum(m, jnp.max(s_p, axis=-1, keepdims=True))
    o = _dot(jnp.exp2(s_d - m).astype(BF16), v_at(n_past, TQ))
    if n_past:
        o = o + _dot(jnp.exp2(s_p - m).astype(BF16), v_at(0, n_past))
    return _normalize(o, ones_lane)


def _bias_lanes(allowed_t, row0):
    n, rows = allowed_t.shape
    pieces = []
    if row0:
        pieces.append(jnp.zeros((row0, rows), F32))
    pieces.append(allowed_t)
    if LANES - row0 - n:
        pieces.append(jnp.zeros((LANES - row0 - n, rows), F32))
    full = jnp.concatenate(pieces, axis=0)
    return ((full.T - 1.0) * (-NEG)).astype(BF16)


def _moba_kernel(q_ref, k_ref, v_ref, z_ref, o_ref, qp_ref, kaug_ref, vaug_ref, bias_ref, osc_ref, *, seq):
    t = pl.program_id(1)
    nb = seq // MOBA_BLOCK
    n_pair = MOBA_HEADS // 2

    @pl.when(t == 0)
    def _():
        for p in range(n_pair):
            sl = slice(p * LANES, (p + 1) * LANES)
            qp_ref[p] = q_ref[:, sl]
            kaug_ref[2 * p] = k_ref[:, sl]
            vaug_ref[2 * p] = v_ref[:, sl]
        rowblk = lax.broadcasted_iota(jnp.int32, (seq, LANES), 0) // MOBA_BLOCK
        lane = lax.broadcasted_iota(jnp.int32, (seq, LANES), 1)
        low = lane < HEAD_DIM
        blk = lax.broadcasted_iota(jnp.int32, (nb, seq), 0)
        own = lax.broadcasted_iota(jnp.int32, (nb, seq), 1) // MOBA_BLOCK
        avg = jnp.where(blk == own, 1.0 / MOBA_BLOCK, 0.0).astype(BF16)

        def build(p, carry):
            k = kaug_ref[2 * p]
            kf = k.astype(F32)
            vf = vaug_ref[2 * p].astype(F32)
            kaug_ref[2 * p] = jnp.where(low, kf, jnp.where(lane - HEAD_DIM == rowblk, 1.0, 0.0)).astype(BF16)
            kaug_ref[2 * p + 1] = jnp.where(low, jnp.where(lane == rowblk, 1.0, 0.0), kf).astype(BF16)
            vaug_ref[2 * p] = jnp.where(low, vf, 1.0).astype(BF16)
            vaug_ref[2 * p + 1] = jnp.where(low, 1.0, vf).astype(BF16)
            km3 = jnp.concatenate(_split3(_dot(avg, k)), axis=0).astype(BF16)
            q_all = qp_ref[p]
            for h in range(2):
                mine = low if h == 0 else jnp.logical_not(low)
                g3 = _dot_nt(km3, jnp.where(mine, q_all, jnp.zeros((), BF16)))
                gate = g3[:nb] + g3[nb:2 * nb] + g3[2 * nb:]
                gate = jnp.where(blk < own, gate, -jnp.inf)
                cnt = _rank_before(gate, nb)
                allowed = jnp.logical_or(jnp.logical_and(cnt < MOBA_TOPK, blk < own), blk == own)
                bias_ref[2 * p + h] = _bias_lanes(jnp.where(allowed, 1.0, 0.0), HEAD_DIM if h == 0 else 0)
            return carry

        lax.fori_loop(0, n_pair, build, 0)

    rows = pl.ds(pl.multiple_of(t * TQ, TQ), TQ)
    lane = lax.broadcasted_iota(jnp.int32, (TQ, LANES), 1)

    for tt in range(nb):
        @pl.when(t == tt)
        def _(tt=tt):
            def body(it, carry):
                for pp in range(MOBA_PAIRS_PER_ITER):
                    p = it * MOBA_PAIRS_PER_ITER + pp
                    q = qp_ref[p, rows, :]
                    outs = []
                    for h in range(2):
                        mine = (lane < HEAD_DIM) if h == 0 else (lane >= HEAD_DIM)
                        i = 2 * p + h
                        outs.append(_prefix_attention(
                            jnp.where(mine, q, bias_ref[i, rows, :]),
                            lambda start, size, i=i: kaug_ref[i, pl.ds(start, size), :],
                            lambda start, size, i=i: vaug_ref[i, pl.ds(start, size), :],
                            tt + 1, HEAD_DIM if h == 0 else 0))
                    osc_ref[p] = jnp.where(lane < HEAD_DIM, outs[0], outs[1])
                return carry

            lax.fori_loop(0, n_pair // MOBA_PAIRS_PER_ITER, body, 0)

    gate_z = _silu(z_ref[...].astype(F32))
    for p in range(n_pair):
        sl = slice(p * LANES, (p + 1) * LANES)
        o_ref[:, sl] = (osc_ref[p] * gate_z[:, sl]).astype(BF16)


def _moba(slab, batch, seq):
    nt = seq // TQ
    w = MOBA_HEADS * HEAD_DIM
    n_pair = MOBA_HEADS // 2
    plane = lambda n: pltpu.VMEM((n, seq, LANES), BF16)
    return pl.pallas_call(
        functools.partial(_moba_kernel, seq=seq),
        grid=(batch, nt),
        in_specs=[
            pl.BlockSpec((seq, w), lambda b, t: (b, _cb('moba_q', w))),
            pl.BlockSpec((seq, w), lambda b, t: (b, _cb('moba_k', w))),
            pl.BlockSpec((seq, w), lambda b, t: (b, _cb('moba_v', w))),
            pl.BlockSpec((TQ, w), lambda b, t: (b * nt + t, _cb('moba_z', w))),
        ],
        out_specs=pl.BlockSpec((TQ, w), lambda b, t: (b * nt + t, 0)),
        out_shape=jax.ShapeDtypeStruct((batch * seq, w), BF16),
        scratch_shapes=[
            plane(n_pair), plane(MOBA_HEADS), plane(MOBA_HEADS), plane(MOBA_HEADS),
            pltpu.VMEM((n_pair, TQ, LANES), F32),
        ],
        compiler_params=pltpu.CompilerParams(dimension_semantics=("parallel", "arbitrary")),
        name="moba",
    )(slab, slab, slab, slab)


GQA_W = 4 * HEAD_DIM
ONES_A, ONES_B = 3 * HEAD_DIM, 0


def _rep2(x, hk):
    lane = lax.broadcasted_iota(jnp.int32, x.shape, 1)
    keep = (lane < HEAD_DIM) == (hk == 0)
    return jnp.where(keep, x, pltpu.roll(x, HEAD_DIM, 1))


def _value_variants(v2):
    low = lax.broadcasted_iota(jnp.int32, v2.shape, 1) < HEAD_DIM
    va = jnp.concatenate([v2, jnp.where(low, v2, 1.0)], axis=1).astype(BF16)
    vb = jnp.concatenate([jnp.where(low, 1.0, v2), v2], axis=1).astype(BF16)
    return va, vb


def _head_mask(g, width, rows=TQ):
    lane = lax.broadcasted_iota(jnp.int32, (rows, width), 1)
    return (lane // HEAD_DIM) == g


SWA_SPAN = TQ + LANES


def _swa_kernel(sink_ref, q_ref, k_ref, v_ref, z_ref, o_ref, krep_ref, va_ref, vb_ref):
    hk = pl.program_id(1)
    t = pl.program_id(2)

    @pl.when(t == 0)
    def _():
        kr = _rep2(k_ref[...].astype(F32), hk).astype(BF16)
        krep_ref[...] = jnp.concatenate([kr, kr], axis=1)
        va_ref[...], vb_ref[...] = _value_variants(_rep2(v_ref[...].astype(F32), hk))

    q = q_ref[...]
    k0 = pl.multiple_of(jnp.maximum(t * TQ - LANES, 0), LANES)
    kw = krep_ref[pl.ds(k0, SWA_SPAN), :]
    dist = (t * TQ + lax.broadcasted_iota(jnp.int32, (TQ, SWA_SPAN), 0)) - (
        k0 + lax.broadcasted_iota(jnp.int32, (TQ, SWA_SPAN), 1))
    mask = jnp.logical_and(dist >= 0, dist < SWA_WINDOW)
    out = jnp.zeros((TQ, GQA_W), F32)
    for g in range(4):
        mine = _head_mask(g, GQA_W)
        vref, ones = (va_ref, ONES_A) if g < 3 else (vb_ref, ONES_B)
        s = jnp.where(mask, _dot_nt(jnp.where(mine, q, jnp.zeros((), BF16)), kw), NEG)
        sink = sink_ref[hk * 4 + g] * LOG2E
        m = jnp.maximum(jnp.max(s, axis=-1, keepdims=True), sink)
        o = _dot(jnp.exp2(s - m).astype(BF16), vref[pl.ds(k0, SWA_SPAN), :])
        den = o[:, ones:ones + 1] + jnp.exp2(sink - m)
        out = jnp.where(mine, o * (1.0 / jnp.maximum(den, TINY)), out)
    o_ref[...] = (out * _silu(z_ref[...].astype(F32))).astype(BF16)


def _swa(slab, sink, batch, seq):
    nt = seq // TQ
    w = GQA_W
    return pl.pallas_call(
        _swa_kernel,
        grid=(batch, SWA_KV_HEADS, nt),
        in_specs=[
            pl.BlockSpec(memory_space=pltpu.SMEM),
            pl.BlockSpec((TQ, w), lambda b, h, t: (b * nt + t, _cb('swa_q', w) + h)),
            pl.BlockSpec((seq, LANES), lambda b, h, t: (b, _cb('swa_k'))),
            pl.BlockSpec((seq, LANES), lambda b, h, t: (b, _cb('swa_v'))),
            pl.BlockSpec((TQ, w), lambda b, h, t: (b * nt + t, _cb('swa_z', w) + h)),
        ],
        out_specs=pl.BlockSpec((TQ, w), lambda b, h, t: (b * nt + t, h)),
        out_shape=jax.ShapeDtypeStruct((batch * seq, SWA_HEADS * HEAD_DIM), BF16),
        scratch_shapes=[pltpu.VMEM((seq, w), BF16), pltpu.VMEM((seq, w), BF16), pltpu.VMEM((seq, w), BF16)],
        compiler_params=pltpu.CompilerParams(dimension_semantics=("parallel", "parallel", "arbitrary")),
        name="swa",
    )(sink, slab, slab, slab, slab)


def _ret_kernel(qk_ref, v_ref, z_ref, o_ref, state_ref):
    c = RET_CHUNK

    @pl.when(pl.program_id(1) == 0)
    def _():
        state_ref[...] = jnp.zeros(state_ref.shape, F32)

    q = qk_ref[:, :RET_HEADS * RET_QK_DIM]
    k = qk_ref[:, RET_HEADS * RET_QK_DIM:]
    ii = lax.broadcasted_iota(jnp.int32, (c, c), 0)
    jj = lax.broadcasted_iota(jnp.int32, (c, c), 1)
    diff = (ii - jj).astype(F32)
    row = lax.broadcasted_iota(jnp.int32, (c, 1), 0).astype(F32)
    lane = lax.broadcasted_iota(jnp.int32, (c, RET_HEADS * RET_QK_DIM), 1)
    for h in range(RET_HEADS):
        log_g = math.log(1.0 - 2.0 ** (-5.0 - h))
        intra = jnp.where(diff >= 0, jnp.exp(jnp.maximum(diff, 0.0) * log_g), 0.0)
        q_dec = jnp.exp((row + 1.0) * log_g)
        k_dec = jnp.exp((c - 1.0 - row) * log_g)
        chunk_dec = math.exp(c * log_g)
        qm = jnp.where((lane // RET_QK_DIM) == h, q, jnp.zeros((), BF16))
        vh = v_ref[:, h * RET_V_DIM:(h + 1) * RET_V_DIM]
        att = _dot_nt(qm, k) * intra
        st = state_ref[h]
        o = _dot(att.astype(BF16), vh) + _dot(qm, st.astype(BF16)) * q_dec
        kd = (k.astype(F32) * k_dec).T.astype(BF16)
        state_ref[h] = st * chunk_dec + _dot(kd, vh)
        mu = jnp.mean(o, axis=-1, keepdims=True)
        var = jnp.mean(jnp.square(o - mu), axis=-1, keepdims=True)
        o = (o - mu) * lax.rsqrt(var + NORM_EPS)
        zh = z_ref[:, h * RET_V_DIM:(h + 1) * RET_V_DIM].astype(F32)
        o_ref[:, h * RET_V_DIM:(h + 1) * RET_V_DIM] = (o * _silu(zh)).astype(BF16)


def _retention(slab, batch, seq):
    nc = seq // RET_CHUNK
    w = RET_HEADS * RET_V_DIM
    return pl.pallas_call(
        _ret_kernel,
        grid=(batch, nc),
        in_specs=[
            pl.BlockSpec((RET_CHUNK, w), lambda b, c: (b * nc + c, _cb('ret_q', w))),
            pl.BlockSpec((RET_CHUNK, w), lambda b, c: (b * nc + c, _cb('ret_v', w))),
            pl.BlockSpec((RET_CHUNK, w), lambda b, c: (b * nc + c, _cb('ret_z', w))),
        ],
        out_specs=pl.BlockSpec((RET_CHUNK, w), lambda b, c: (b * nc + c, 0)),
        out_shape=jax.ShapeDtypeStruct((batch * seq, w), BF16),
        scratch_shapes=[pltpu.VMEM((RET_HEADS, RET_HEADS * RET_QK_DIM, RET_V_DIM), F32)],
        compiler_params=pltpu.CompilerParams(dimension_semantics=("parallel", "arbitrary")),
        name="retention",
    )(slab, slab, slab)


def _compress_kernel(r_ref, w1_ref, w2_ref, pos_ref, cos_ref, sa_ref, sb_ref, o_ref):
    kind = pl.program_id(0) // NSA_KV_HEADS
    half = NSA_CMP_STRIDE * HEAD_DIM
    r = r_ref[...]
    n_rows = r.shape[0]
    lo = _dot(r, w1_ref[:half, :])
    hi = _dot(r, w1_ref[half:, :])
    pos = _dot(pos_ref[...], w1_ref[...])[0:1, :]
    pre = lo + pltpu.roll(hi, n_rows - 1, 0) + pos
    y = _dot(jax.nn.gelu(pre).astype(BF16), w2_ref[...])
    roped = _rope_chunk(y, cos_ref[...], sa_ref[...], sb_ref[...])
    o_ref[...] = jnp.where(kind == 0, roped, y)


def _compress(r, w1, w2, pos8, cos_c, sa_c, sb_c, batch):
    n_rows = r.shape[2]
    flat = NSA_CMP_LEN * HEAD_DIM
    return pl.pallas_call(
        _compress_kernel,
        grid=(2 * NSA_KV_HEADS, batch),
        in_specs=[
            pl.BlockSpec((None, None, n_rows, flat // 2), lambda i, b: (i, b, 0, 0)),
            pl.BlockSpec((None, flat, NSA_CMP_HIDDEN), lambda i, b: (i // NSA_KV_HEADS, 0, 0)),
            pl.BlockSpec((None, NSA_CMP_HIDDEN, LANES), lambda i, b: (i // NSA_KV_HEADS, 0, 0)),
            pl.BlockSpec((8, flat), lambda i, b: (0, 0)),
            pl.BlockSpec((n_rows, LANES), lambda i, b: (0, 0)),
            pl.BlockSpec((n_rows, LANES), lambda i, b: (0, 0)),
            pl.BlockSpec((n_rows, LANES), lambda i, b: (0, 0)),
        ],
        out_specs=pl.BlockSpec((None, None, n_rows, LANES), lambda i, b: (i, b, 0, 0)),
        out_shape=jax.ShapeDtypeStruct((2 * NSA_KV_HEADS, batch, n_rows, LANES), F32),
        compiler_params=pltpu.CompilerParams(dimension_semantics=("parallel", "parallel")),
        name="nsa_compress",
    )(r, w1, w2, pos8, cos_c, sa_c, sb_c)


NSA_WIN_SPAN = TQ + NSA_WINDOW
SEL_LANE0 = 64


def _nsa_kernel(q_ref, ks_ref, vs_ref, kw_ref, vw_ref, kc_ref, vc_ref, g_ref, z_ref, o_ref,
                ksa_ref, ksb_ref, vsa_ref, vsb_ref, kwr_ref, vwa_ref, vwb_ref, kcr_ref, vcr_ref,
                ocmp_ref, bias_ref, *, seq):
    hk = pl.program_id(1)
    t = pl.program_id(2)
    n_slc = seq // NSA_SLC_BLOCK
    n_cmp = (seq - NSA_CMP_LEN) // NSA_CMP_STRIDE + 1
    n_crow = kc_ref.shape[0]
    w = GQA_W

    @pl.when(t == 0)
    def _():
        ks = _rep2(ks_ref[...].astype(F32), hk).astype(BF16)
        rowblk = lax.broadcasted_iota(jnp.int32, (seq, LANES), 0) // NSA_SLC_BLOCK
        lane = lax.broadcasted_iota(jnp.int32, (seq, LANES), 1)
        onehot = jnp.where(lane - SEL_LANE0 == rowblk, 1.0, 0.0).astype(BF16)
        ksa_ref[...] = jnp.concatenate([ks, onehot], axis=1)
        ksb_ref[...] = jnp.concatenate([onehot, ks], axis=1)
        vsa_ref[...], vsb_ref[...] = _value_variants(_rep2(vs_ref[...].astype(F32), hk))
        kwin = _rep2(kw_ref[...].astype(F32), hk).astype(BF16)
        kwr_ref[...] = jnp.concatenate([kwin, kwin], axis=1)
        vwa_ref[...], vwb_ref[...] = _value_variants(_rep2(vw_ref[...].astype(F32), hk))
        kc = kc_ref[...]
        kc = (kc + pltpu.roll(kc, HEAD_DIM, 1)).astype(BF16)
        kcr_ref[...] = jnp.concatenate([kc, kc], axis=1)
        vc = vc_ref[...]
        vc = (vc + pltpu.roll(vc, HEAD_DIM, 1)).astype(BF16)
        vcr_ref[...] = jnp.concatenate([vc, vc], axis=1)

        q_all = q_ref[...]
        pos_all = lax.broadcasted_iota(jnp.int32, (seq, 1), 0)
        ci = lax.broadcasted_iota(jnp.int32, (1, n_crow), 1)
        cmask = jnp.logical_and(ci * NSA_CMP_STRIDE + NSA_CMP_LEN - 1 <= pos_all, ci < n_cmp)
        kcr, vcr = kcr_ref[...], vcr_ref[...]
        p_sum = jnp.zeros((seq, n_crow), F32)
        o_cmp = jnp.zeros((seq, w), F32)
        for g in range(4):
            mask_g = _head_mask(g, w, seq)
            s = jnp.where(cmask, _dot_nt(jnp.where(mask_g, q_all, jnp.zeros((), BF16)), kcr), NEG)
            m = jnp.max(s, axis=-1, keepdims=True)
            e = jnp.where(cmask, jnp.exp2(s - m), 0.0)
            p = e * (1.0 / jnp.maximum(jnp.sum(e, axis=-1, keepdims=True), TINY))
            p_sum = p_sum + p
            o_cmp = jnp.where(mask_g, _dot(p.astype(BF16), vcr), o_cmp)
        ocmp_ref[...] = o_cmp

        c0 = lax.broadcasted_iota(jnp.int32, (n_slc, n_crow), 1) * NSA_CMP_STRIDE
        s0 = lax.broadcasted_iota(jnp.int32, (n_slc, n_crow), 0) * NSA_SLC_BLOCK
        ov = jnp.minimum(c0 + NSA_CMP_LEN, s0 + NSA_SLC_BLOCK) - jnp.maximum(c0, s0)
        ov = (jnp.maximum(ov, 0).astype(F32) * (1.0 / NSA_CMP_LEN)).astype(BF16)
        imp = _dot_nt(jnp.concatenate([ov, ov, ov], axis=1),
                      jnp.concatenate(_split3(p_sum), axis=1).astype(BF16))
        j_iota = lax.broadcasted_iota(jnp.int32, (n_slc, seq), 0)
        q_blk = lax.broadcasted_iota(jnp.int32, (n_slc, seq), 1) // NSA_SLC_BLOCK
        forced = jnp.logical_or(j_iota == 0, jnp.logical_or(j_iota == q_blk, j_iota == q_blk - 1))
        imp = jnp.where(forced, jnp.inf, imp)
        imp = jnp.where(j_iota <= q_blk, imp, -jnp.inf)
        cnt = _rank_before(imp, n_slc)
        allowed = jnp.logical_and(cnt < min(NSA_SLC_TOPN, n_slc), j_iota <= q_blk)
        bias_ref[...] = _bias_lanes(jnp.where(allowed, 1.0, 0.0), SEL_LANE0)

    rows = pl.ds(pl.multiple_of(t * TQ, TQ), TQ)
    q = q_ref[rows, :]
    bias = bias_ref[rows, :]
    zero = jnp.zeros((), BF16)
    masks = [_head_mask(g, w) for g in range(4)]
    qms = [jnp.where(masks[g], q, zero) for g in range(4)]
    qaugs = []
    for g in range(4):
        half_mask = _head_mask(g % 2, LANES)
        if g < 2:
            qaugs.append(jnp.concatenate([jnp.where(half_mask, q[:, :LANES], zero), bias], axis=1))
        else:
            qaugs.append(jnp.concatenate([bias, jnp.where(half_mask, q[:, LANES:], zero)], axis=1))

    gates = jax.nn.sigmoid(g_ref[...].astype(F32))
    g_hi = gates.astype(BF16)
    g_lo = (gates - g_hi.astype(F32)).astype(BF16)
    er = lax.broadcasted_iota(jnp.int32, (2 * LANES, 3 * w), 0) % LANES
    ec = lax.broadcasted_iota(jnp.int32, (2 * LANES, 3 * w), 1)
    expand = jnp.where(er == (ec // w) * NSA_HEADS + hk * 4 + (ec % w) // HEAD_DIM, 1.0, 0.0).astype(BF16)
    gx = _dot(jnp.concatenate([g_hi, g_lo], axis=1), expand)
    o_cmp = gx[:, :w] * ocmp_ref[rows, :]
    gate_z = _silu(z_ref[...].astype(F32))

    for tt in range(seq // TQ):
        @pl.when(t == tt)
        def _(tt=tt):
            k0 = max(tt * TQ - NSA_WINDOW, 0)
            kwin = kwr_ref[k0:k0 + NSA_WIN_SPAN, :]
            dist = (tt * TQ + lax.broadcasted_iota(jnp.int32, (TQ, 1), 0)) - (
                k0 + lax.broadcasted_iota(jnp.int32, (1, NSA_WIN_SPAN), 1))
            wmask = jnp.logical_and(dist >= 0, dist < NSA_WINDOW)
            o_win = jnp.zeros((TQ, w), F32)
            o_slc = jnp.zeros((TQ, w), F32)
            for g in range(4):
                vref, ones = (vwa_ref, ONES_A) if g < 3 else (vwb_ref, ONES_B)
                s = jnp.where(wmask, _dot_nt(qms[g], kwin), NEG)
                m = jnp.max(s, axis=-1, keepdims=True)
                o = _dot(jnp.exp2(s - m).astype(BF16), vref[k0:k0 + NSA_WIN_SPAN, :])
                o_win = jnp.where(masks[g], _normalize(o, ones), o_win)

                kref = ksa_ref if g < 2 else ksb_ref
                vref, ones = (vsa_ref, ONES_A) if g < 3 else (vsb_ref, ONES_B)
                o = _prefix_attention(
                    qaugs[g],
                    lambda start, size, kref=kref: kref[start:start + size, :],
                    lambda start, size, vref=vref: vref[start:start + size, :],
                    tt + 1, ones)
                o_slc = jnp.where(masks[g], o, o_slc)
            o = o_cmp + gx[:, w:2 * w] * o_slc + gx[:, 2 * w:] * o_win
            o_ref[...] = (o * gate_z).astype(BF16)


def _nsa(slab, cmp_kv, batch, seq):
    nt = seq // TQ
    w = GQA_W
    n_crow = cmp_kv.shape[2]
    kv_spec = lambda name: pl.BlockSpec((seq, LANES), lambda b, h, t: (b, _cb(name)))
    big = pltpu.VMEM((seq, w), BF16)
    return pl.pallas_call(
        functools.partial(_nsa_kernel, seq=seq),
        grid=(batch, NSA_KV_HEADS, nt),
        in_specs=[
            pl.BlockSpec((seq, w), lambda b, h, t: (b, _cb('nsa_q', w) + h)),
            kv_spec('nsa_ks'), kv_spec('nsa_vs'), kv_spec('nsa_kw'), kv_spec('nsa_vw'),
            pl.BlockSpec((None, None, n_crow, LANES), lambda b, h, t: (h, b, 0, 0)),
            pl.BlockSpec((None, None, n_crow, LANES), lambda b, h, t: (NSA_KV_HEADS + h, b, 0, 0)),
            pl.BlockSpec((TQ, LANES), lambda b, h, t: (b * nt + t, _cb('nsa_g'))),
            pl.BlockSpec((TQ, w), lambda b, h, t: (b * nt + t, _cb('nsa_z', w) + h)),
        ],
        out_specs=pl.BlockSpec((TQ, w), lambda b, h, t: (b * nt + t, h)),
        out_shape=jax.ShapeDtypeStruct((batch * seq, NSA_HEADS * HEAD_DIM), BF16),
        scratch_shapes=[
            big, big, big, big, big, big, big,
            pltpu.VMEM((n_crow, w), BF16), pltpu.VMEM((n_crow, w), BF16),
            pltpu.VMEM((seq, w), F32), pltpu.VMEM((seq, LANES), BF16),
        ],
        compiler_params=pltpu.CompilerParams(dimension_semantics=("parallel", "parallel", "arbitrary")),
        name="nsa",
    )(slab, slab, slab, slab, slab, cmp_kv, cmp_kv, slab, slab)


def _merge_kernel(x_ref, g_ref, oa_ref, ob_ref, oc_ref, od_ref, wb_ref, wo_ref, fg_ref, o_ref, *, final):
    merged = None
    row = 0
    for i, br in enumerate((oa_ref, ob_ref, oc_ref, od_ref)):
        width = br.shape[1]
        y = _dot(br[...], wb_ref[row:row + width, :])
        gate = jax.nn.sigmoid(g_ref[:, i * D_MODEL:(i + 1) * D_MODEL].astype(F32))
        merged = gate * y if merged is None else merged + gate * y
        row += width
    x = x_ref[...] + _dot(merged.astype(BF16), wo_ref[...])
    if final:
        ms = jnp.mean(x * x, axis=-1, keepdims=True)
        x = x * lax.rsqrt(ms + NORM_EPS) * fg_ref[...]
    o_ref[...] = x


def _merge(x2, slab, outs, w_branch, w_out, final_g, final):
    t = x2.shape[0]
    tm = MERGE_TM
    d_branch = w_branch.shape[0]
    row_spec = lambda width: pl.BlockSpec((tm, width), lambda i: (i, 0))
    return pl.pallas_call(
        functools.partial(_merge_kernel, final=final),
        grid=(t // tm,),
        in_specs=[
            row_spec(D_MODEL),
            pl.BlockSpec((tm, N_BRANCH * D_MODEL), lambda i: (i, _cb('merge_g', N_BRANCH * D_MODEL))),
            row_spec(outs[0].shape[1]), row_spec(outs[1].shape[1]),
            row_spec(outs[2].shape[1]), row_spec(outs[3].shape[1]),
            pl.BlockSpec((d_branch, D_MODEL), lambda i: (0, 0)),
            pl.BlockSpec((D_MODEL, D_MODEL), lambda i: (0, 0)),
            pl.BlockSpec((1, D_MODEL), lambda i: (0, 0)),
        ],
        out_specs=row_spec(D_MODEL),
        out_shape=jax.ShapeDtypeStruct((t, D_MODEL), F32),
        compiler_params=pltpu.CompilerParams(dimension_semantics=("parallel",)),
        name="merge_out",
    )(x2, slab, *outs, w_branch, w_out, final_g.reshape(1, D_MODEL))


def _rope_tables(positions):
    inv = 1.0 / (ROPE_THETA ** (jnp.arange(0, HEAD_DIM, 2, dtype=F32) / HEAD_DIM))
    ang = positions.astype(F32)[:, None] * inv[None, :]
    cos, sin = jnp.cos(ang), jnp.sin(ang)
    zero = jnp.zeros_like(sin)
    cos_t = jnp.tile(cos, (1, 4))
    sa_t = jnp.tile(jnp.concatenate([-sin, zero], axis=1), (1, 2))
    sb_t = jnp.tile(jnp.concatenate([zero, sin], axis=1), (1, 2))
    return cos_t, sa_t, sb_t


def kernel(x, norm_g, w_in, w_branch, w_out, swa_sink, nsa_cmp_pos, nsa_w_ck1, nsa_w_ck2,
           nsa_w_cv1, nsa_w_cv2, final_norm_g):
    batch, seq, d = x.shape
    depth = w_in.shape[0]
    assert d == D_MODEL and seq % TQ == 0 and seq % RET_CHUNK == 0 and (batch * seq) % MERGE_TM == 0
    assert seq >= NSA_WIN_SPAN and seq % (8 * NSA_CMP_STRIDE) == 0 and (seq // MOBA_BLOCK) % 8 == 0

    cos_t, sa_t, sb_t = _rope_tables(jnp.arange(seq))
    n_crow = seq // NSA_CMP_STRIDE
    cos_c, sa_c, sb_c = _rope_tables(jnp.arange(n_crow) * NSA_CMP_STRIDE + NSA_CMP_LEN - 1)

    x2 = x.reshape(batch * seq, d)
    for l in range(depth):
        slab, kcvc = _proj_in(x2, norm_g[l], _slab_weights(w_in[l]), cos_t, sa_t, sb_t, seq)

        groups = kcvc.reshape(2 * NSA_KV_HEADS, batch, n_crow, NSA_CMP_STRIDE * HEAD_DIM)
        w1 = jnp.stack([nsa_w_ck1[l], nsa_w_cv1[l]]).astype(BF16)
        w2 = jnp.pad(jnp.stack([nsa_w_ck2[l], nsa_w_cv2[l]]), ((0, 0), (0, 0), (0, LANES - HEAD_DIM))).astype(BF16)
        pos8 = jnp.broadcast_to(nsa_cmp_pos[l].reshape(1, -1), (8, NSA_CMP_LEN * HEAD_DIM)).astype(BF16)
        cmp_kv = _compress(groups, w1, w2, pos8, cos_c, sa_c, sb_c, batch)

        outs = (
            _moba(slab, batch, seq),
            _swa(slab, swa_sink[l], batch, seq),
            _retention(slab, batch, seq),
            _nsa(slab, cmp_kv, batch, seq),
        )
        x2 = _merge(x2, slab, outs, w_branch[l].astype(BF16), w_out[l].astype(BF16),
                    final_norm_g, final=(l == depth - 1))
    return x2.reshape(batch, seq, d)
```

```python
import functools
import math

import numpy as np
import jax
import jax.numpy as jnp
from jax import lax
from jax.experimental import pallas as pl
from jax.experimental.pallas import tpu as pltpu

F32 = jnp.float32
BF16 = jnp.bfloat16

D_MODEL = 1024
HEAD_DIM = 64
ROPE_THETA = 10000.0
NORM_EPS = 1e-6
TINY = 1e-30
N_BRANCH = 4
NEG = -1e30
LOG2E = 1.4426950408889634

MOBA_HEADS = 8
MOBA_BLOCK = 256
MOBA_TOPK = 3

SWA_HEADS = 8
SWA_KV_HEADS = 2
SWA_WINDOW = 128

RET_HEADS = 4
RET_QK_DIM = 64
RET_V_DIM = 128

NSA_HEADS = 8
NSA_KV_HEADS = 2
NSA_CMP_LEN = 32
NSA_CMP_STRIDE = 16
NSA_CMP_HIDDEN = 256
NSA_SLC_BLOCK = 64
NSA_SLC_TOPN = 16
NSA_WINDOW = 512

LANES = 128
MXU_N = 256
TQ = 256
MOBA_PAIRS_PER_ITER = 2
PROJ_TM = 2048
PROJ_TN = 1024
MERGE_TM = 512
RET_CHUNK = 256

_IN_SPLITS = (
    ('moba_q', 512), ('moba_k', 512), ('moba_v', 512), ('moba_z', 512),
    ('swa_q', 512), ('swa_k', 128), ('swa_v', 128), ('swa_z', 512),
    ('ret_q', 256), ('ret_k', 256), ('ret_v', 512), ('ret_z', 512),
    ('nsa_q', 512), ('nsa_kc', 128), ('nsa_vc', 128),
    ('nsa_ks', 128), ('nsa_vs', 128), ('nsa_kw', 128), ('nsa_vw', 128),
    ('nsa_g', 24), ('nsa_z', 512),
    ('merge_g', 4096),
)
_WIDTH = dict(_IN_SPLITS)
_SLAB = (
    ('merge_g', 4096),
    ('moba_q', 512), ('moba_k', 512), ('swa_q', 512), ('nsa_q', 512),
    ('ret_q', 256), ('ret_k', 256),
    ('swa_k', 128), ('nsa_ks', 128), ('nsa_kw', 128), (None, 128),
    ('moba_v', 512), ('moba_z', 512), ('swa_z', 512), ('ret_v', 512), ('ret_z', 512), ('nsa_z', 512),
    ('swa_v', 128), ('nsa_vs', 128), ('nsa_vw', 128), ('nsa_g', 128),
    ('nsa_kc', 128), ('nsa_vc', 128), (None, 256),
)
_COL_SCALE = {'moba_q': 0.125 * LOG2E, 'swa_q': 0.125 * LOG2E, 'nsa_q': 0.125 * LOG2E, 'ret_k': 0.125}


def _slab_layout():
    src_off, off = {}, 0
    for name, w in _IN_SPLITS:
        src_off[name] = off
        off += w
    col, pos = {}, 0
    for name, w in _SLAB:
        if name is not None:
            col[name] = pos
        pos += w
    return src_off, col, pos


_SRC_OFF, COL, D_SLAB = _slab_layout()


def _slab_weights(w):
    parts = []
    for name, width in _SLAB:
        if name is None:
            parts.append(jnp.zeros((w.shape[0], width), BF16))
            continue
        blk = w[:, _SRC_OFF[name]:_SRC_OFF[name] + _WIDTH[name]]
        if name in _COL_SCALE:
            blk = blk * _COL_SCALE[name]
        parts.append(jnp.pad(blk.astype(BF16), ((0, 0), (0, width - _WIDTH[name]))))
    return jnp.concatenate(parts, axis=1)
ROPE_COL_LO, ROPE_COL_HI = COL['moba_q'], COL['moba_v']
assert ROPE_COL_LO % PROJ_TN == 0 and ROPE_COL_HI % PROJ_TN == 0 and D_SLAB % PROJ_TN == 0
KCVC_TILE, KCVC_OFF = divmod(COL['nsa_kc'], PROJ_TN)


def _cb(name, width=LANES):
    assert COL[name] % width == 0
    return COL[name] // width


NT = (((1,), (1,)), ((), ()))


def _dot(a, b):
    return jnp.dot(a, b, preferred_element_type=F32)


def _dot_nt(a, b):
    return lax.dot_general(a, b, NT, preferred_element_type=F32)


def _silu(z):
    return z * jax.nn.sigmoid(z)


def _rope_chunk(y, cos, sin_a, sin_b):
    return y * cos + pltpu.roll(y, 96, 1) * sin_a + pltpu.roll(y, 32, 1) * sin_b


def _split3(x):
    x1 = x.astype(BF16).astype(F32)
    x2 = (x - x1).astype(BF16).astype(F32)
    x3 = (x - x1 - x2).astype(BF16).astype(F32)
    return x1, x2, x3


def _proj_kernel(x_ref, g_ref, w_ref, cos_ref, sa_ref, sb_ref, o_ref, kcvc_ref, h_ref):
    j = pl.program_id(1)

    @pl.when(j == 0)
    def _():
        x = x_ref[...]
        ms = jnp.mean(x * x, axis=-1, keepdims=True)
        h_ref[...] = (x * lax.rsqrt(ms + NORM_EPS) * g_ref[...]).astype(BF16)

    is_rope = jnp.logical_and(j >= ROPE_COL_LO // PROJ_TN, j < ROPE_COL_HI // PROJ_TN)

    @pl.when(is_rope)
    def _():
        cos, sa, sb = cos_ref[...], sa_ref[...], sb_ref[...]
        for c in range(PROJ_TN // MXU_N):
            y = _dot(h_ref[...], w_ref[:, c * MXU_N:(c + 1) * MXU_N])
            for half in range(MXU_N // LANES):
                lo = c * MXU_N + half * LANES
                o_ref[:, lo:lo + LANES] = _rope_chunk(y[:, half * LANES:(half + 1) * LANES], cos, sa, sb).astype(BF16)

    @pl.when(jnp.logical_not(is_rope))
    def _():
        y = _dot(h_ref[...], w_ref[...])
        o_ref[...] = y.astype(BF16)

        @pl.when(j == KCVC_TILE)
        def _():
            for c in range(2 * NSA_KV_HEADS):
                lo = KCVC_OFF + c * HEAD_DIM
                kcvc_ref[c] = y[:, lo:lo + HEAD_DIM].astype(BF16)


def _proj_in(x2, norm_g, w_slab, cos_t, sa_t, sb_t, seq):
    t = x2.shape[0]
    tm = min(PROJ_TM, seq)
    per_seq = seq // tm
    return pl.pallas_call(
        _proj_kernel,
        grid=(t // tm, D_SLAB // PROJ_TN),
        in_specs=[
            pl.BlockSpec((tm, D_MODEL), lambda i, j: (i, 0)),
            pl.BlockSpec((1, D_MODEL), lambda i, j: (0, 0)),
            pl.BlockSpec((D_MODEL, PROJ_TN), lambda i, j: (0, j)),
            pl.BlockSpec((tm, LANES), lambda i, j: (i % per_seq, 0)),
            pl.BlockSpec((tm, LANES), lambda i, j: (i % per_seq, 0)),
            pl.BlockSpec((tm, LANES), lambda i, j: (i % per_seq, 0)),
        ],
        out_specs=[
            pl.BlockSpec((tm, PROJ_TN), lambda i, j: (i, j)),
            pl.BlockSpec((2 * NSA_KV_HEADS, tm, HEAD_DIM), lambda i, j: (0, i, 0)),
        ],
        out_shape=[
            jax.ShapeDtypeStruct((t, D_SLAB), BF16),
            jax.ShapeDtypeStruct((2 * NSA_KV_HEADS, t, HEAD_DIM), BF16),
        ],
        scratch_shapes=[pltpu.VMEM((tm, D_MODEL), BF16)],
        compiler_params=pltpu.CompilerParams(dimension_semantics=("parallel", "arbitrary")),
        name="proj_in",
    )(x2, norm_g.reshape(1, D_MODEL), w_slab, cos_t, sa_t, sb_t)


def _rank_before(scores, n):
    j_iota = lax.broadcasted_iota(jnp.int32, scores.shape, 0)
    cnt = jnp.zeros(scores.shape, F32)
    for jp in range(n):
        r = scores[jp:jp + 1, :]
        ahead = jnp.logical_or(r > scores, jnp.logical_and(r == scores, jp < j_iota))
        cnt = cnt + jnp.where(ahead, 1.0, 0.0)
    return cnt


def _normalize(o, ones_lane):
    return o * (1.0 / jnp.maximum(o[:, ones_lane:ones_lane + 1], TINY))


def _prefix_attention(qaug, k_at, v_at, n_tiles, ones_lane):
    n_past = (n_tiles - 1) * TQ
    row = lax.broadcasted_iota(jnp.int32, (TQ, TQ), 0)
    col = lax.broadcasted_iota(jnp.int32, (TQ, TQ), 1)
    s_d = jnp.where(col <= row, _dot_nt(qaug, k_at(n_past, TQ)), NEG)
    m = jnp.max(s_d, axis=-1, keepdims=True)
    if n_past:
        s_p = _dot_nt(qaug, k_at(0, n_past))
        m = jnp.maximum(m, jnp.max(s_p, axis=-1, keepdims=True))
    o = _dot(jnp.exp2(s_d - m).astype(BF16), v_at(n_past, TQ))
    if n_past:
        o = o + _dot(jnp.exp2(s_p - m).astype(BF16), v_at(0, n_past))
    return _normalize(o, ones_lane)


def _bias_lanes(allowed_t, row0):
    n, rows = allowed_t.shape
    pieces = []
    if row0:
        pieces.append(jnp.zeros((row0, rows), F32))
    pieces.append(allowed_t)
    if LANES - row0 - n:
        pieces.append(jnp.zeros((LANES - row0 - n, rows), F32))
    full = jnp.concatenate(pieces, axis=0)
    return ((full.T - 1.0) * (-NEG)).astype(BF16)


def _moba_kernel(q_ref, k_ref, v_ref, z_ref, o_ref, qp_ref, kaug_ref, vaug_ref, bias_ref, osc_ref, *, seq):
    t = pl.program_id(1)
    nb = seq // MOBA_BLOCK
    n_pair = MOBA_HEADS // 2

    @pl.when(t == 0)
    def _():
        for p in range(n_pair):
            sl = slice(p * LANES, (p + 1) * LANES)
            qp_ref[p] = q_ref[:, sl]
            kaug_ref[2 * p] = k_ref[:, sl]
            vaug_ref[2 * p] = v_ref[:, sl]
        rowblk = lax.broadcasted_iota(jnp.int32, (seq, LANES), 0) // MOBA_BLOCK
        lane = lax.broadcasted_iota(jnp.int32, (seq, LANES), 1)
        low = lane < HEAD_DIM
        blk = lax.broadcasted_iota(jnp.int32, (nb, seq), 0)
        own = lax.broadcasted_iota(jnp.int32, (nb, seq), 1) // MOBA_BLOCK
        avg = jnp.where(blk == own, 1.0 / MOBA_BLOCK, 0.0).astype(BF16)

        def build(p, carry):
            k = kaug_ref[2 * p]
            kf = k.astype(F32)
            vf = vaug_ref[2 * p].astype(F32)
            kaug_ref[2 * p] = jnp.where(low, kf, jnp.where(lane - HEAD_DIM == rowblk, 1.0, 0.0)).astype(BF16)
            kaug_ref[2 * p + 1] = jnp.where(low, jnp.where(lane == rowblk, 1.0, 0.0), kf).astype(BF16)
            vaug_ref[2 * p] = jnp.where(low, vf, 1.0).astype(BF16)
            vaug_ref[2 * p + 1] = jnp.where(low, 1.0, vf).astype(BF16)
            km3 = jnp.concatenate(_split3(_dot(avg, k)), axis=0).astype(BF16)
            q_all = qp_ref[p]
            for h in range(2):
                mine = low if h == 0 else jnp.logical_not(low)
                g3 = _dot_nt(km3, jnp.where(mine, q_all, jnp.zeros((), BF16)))
                gate = g3[:nb] + g3[nb:2 * nb] + g3[2 * nb:]
                gate = jnp.where(blk < own, gate, -jnp.inf)
                cnt = _rank_before(gate, nb)
                allowed = jnp.logical_or(jnp.logical_and(cnt < MOBA_TOPK, blk < own), blk == own)
                bias_ref[2 * p + h] = _bias_lanes(jnp.where(allowed, 1.0, 0.0), HEAD_DIM if h == 0 else 0)
            return carry

        lax.fori_loop(0, n_pair, build, 0)

    rows = pl.ds(pl.multiple_of(t * TQ, TQ), TQ)
    lane = lax.broadcasted_iota(jnp.int32, (TQ, LANES), 1)

    for tt in range(nb):
        @pl.when(t == tt)
        def _(tt=tt):
            def body(it, carry):
                for pp in range(MOBA_PAIRS_PER_ITER):
                    p = it * MOBA_PAIRS_PER_ITER + pp
                    q = qp_ref[p, rows, :]
                    outs = []
                    for h in range(2):
                        mine = (lane < HEAD_DIM) if h == 0 else (lane >= HEAD_DIM)
                        i = 2 * p + h
                        outs.append(_prefix_attention(
                            jnp.where(mine, q, bias_ref[i, rows, :]),
                            lambda start, size, i=i: kaug_ref[i, pl.ds(start, size), :],
                            lambda start, size, i=i: vaug_ref[i, pl.ds(start, size), :],
                            tt + 1, HEAD_DIM if h == 0 else 0))
                    osc_ref[p] = jnp.where(lane < HEAD_DIM, outs[0], outs[1])
                return carry

            lax.fori_loop(0, n_pair // MOBA_PAIRS_PER_ITER, body, 0)

    gate_z = _silu(z_ref[...].astype(F32))
    for p in range(n_pair):
        sl = slice(p * LANES, (p + 1) * LANES)
        o_ref[:, sl] = (osc_ref[p] * gate_z[:, sl]).astype(BF16)


def _moba(slab, batch, seq):
    nt = seq // TQ
    w = MOBA_HEADS * HEAD_DIM
    n_pair = MOBA_HEADS // 2
    plane = lambda n: pltpu.VMEM((n, seq, LANES), BF16)
    return pl.pallas_call(
        functools.partial(_moba_kernel, seq=seq),
        grid=(batch, nt),
        in_specs=[
            pl.BlockSpec((seq, w), lambda b, t: (b, _cb('moba_q', w))),
            pl.BlockSpec((seq, w), lambda b, t: (b, _cb('moba_k', w))),
            pl.BlockSpec((seq, w), lambda b, t: (b, _cb('moba_v', w))),
            pl.BlockSpec((TQ, w), lambda b, t: (b * nt + t, _cb('moba_z', w))),
        ],
        out_specs=pl.BlockSpec((TQ, w), lambda b, t: (b * nt + t, 0)),
        out_shape=jax.ShapeDtypeStruct((batch * seq, w), BF16),
        scratch_shapes=[
            plane(n_pair), plane(MOBA_HEADS), plane(MOBA_HEADS), plane(MOBA_HEADS),
            pltpu.VMEM((n_pair, TQ, LANES), F32),
        ],
        compiler_params=pltpu.CompilerParams(dimension_semantics=("parallel", "arbitrary")),
        name="moba",
    )(slab, slab, slab, slab)


GQA_W = 4 * HEAD_DIM
ONES_A, ONES_B = 3 * HEAD_DIM, 0


def _rep2(x, hk):
    lane = lax.broadcasted_iota(jnp.int32, x.shape, 1)
    keep = (lane < HEAD_DIM) == (hk == 0)
    return jnp.where(keep, x, pltpu.roll(x, HEAD_DIM, 1))


def _value_variants(v2):
    low = lax.broadcasted_iota(jnp.int32, v2.shape, 1) < HEAD_DIM
    va = jnp.concatenate([v2, jnp.where(low, v2, 1.0)], axis=1).astype(BF16)
    vb = jnp.concatenate([jnp.where(low, 1.0, v2), v2], axis=1).astype(BF16)
    return va, vb


def _head_mask(g, width, rows=TQ):
    lane = lax.broadcasted_iota(jnp.int32, (rows, width), 1)
    return (lane // HEAD_DIM) == g


SWA_SPAN = TQ + LANES


def _swa_kernel(sink_ref, q_ref, k_ref, v_ref, z_ref, o_ref, krep_ref, va_ref, vb_ref):
    hk = pl.program_id(1)
    t = pl.program_id(2)

    @pl.when(t == 0)
    def _():
        kr = _rep2(k_ref[...].astype(F32), hk).astype(BF16)
        krep_ref[...] = jnp.concatenate([kr, kr], axis=1)
        va_ref[...], vb_ref[...] = _value_variants(_rep2(v_ref[...].astype(F32), hk))

    q = q_ref[...]
    k0 = pl.multiple_of(jnp.maximum(t * TQ - LANES, 0), LANES)
    kw = krep_ref[pl.ds(k0, SWA_SPAN), :]
    dist = (t * TQ + lax.broadcasted_iota(jnp.int32, (TQ, SWA_SPAN), 0)) - (
        k0 + lax.broadcasted_iota(jnp.int32, (TQ, SWA_SPAN), 1))
    mask = jnp.logical_and(dist >= 0, dist < SWA_WINDOW)
    out = jnp.zeros((TQ, GQA_W), F32)
    for g in range(4):
        mine = _head_mask(g, GQA_W)
        vref, ones = (va_ref, ONES_A) if g < 3 else (vb_ref, ONES_B)
        s = jnp.where(mask, _dot_nt(jnp.where(mine, q, jnp.zeros((), BF16)), kw), NEG)
        sink = sink_ref[hk * 4 + g] * LOG2E
        m = jnp.maximum(jnp.max(s, axis=-1, keepdims=True), sink)
        o = _dot(jnp.exp2(s - m).astype(BF16), vref[pl.ds(k0, SWA_SPAN), :])
        den = o[:, ones:ones + 1] + jnp.exp2(sink - m)
        out = jnp.where(mine, o * (1.0 / jnp.maximum(den, TINY)), out)
    o_ref[...] = (out * _silu(z_ref[...].astype(F32))).astype(BF16)


def _swa(slab, sink, batch, seq):
    nt = seq // TQ
    w = GQA_W
    return pl.pallas_call(
        _swa_kernel,
        grid=(batch, SWA_KV_HEADS, nt),
        in_specs=[
            pl.BlockSpec(memory_space=pltpu.SMEM),
            pl.BlockSpec((TQ, w), lambda b, h, t: (b * nt + t, _cb('swa_q', w) + h)),
            pl.BlockSpec((seq, LANES), lambda b, h, t: (b, _cb('swa_k'))),
            pl.BlockSpec((seq, LANES), lambda b, h, t: (b, _cb('swa_v'))),
            pl.BlockSpec((TQ, w), lambda b, h, t: (b * nt + t, _cb('swa_z', w) + h)),
        ],
        out_specs=pl.BlockSpec((TQ, w), lambda b, h, t: (b * nt + t, h)),
        out_shape=jax.ShapeDtypeStruct((batch * seq, SWA_HEADS * HEAD_DIM), BF16),
        scratch_shapes=[pltpu.VMEM((seq, w), BF16), pltpu.VMEM((seq, w), BF16), pltpu.VMEM((seq, w), BF16)],
        compiler_params=pltpu.CompilerParams(dimension_semantics=("parallel", "parallel", "arbitrary")),
        name="swa",
    )(sink, slab, slab, slab, slab)


def _ret_kernel(qk_ref, v_ref, z_ref, o_ref, state_ref):
    c = RET_CHUNK

    @pl.when(pl.program_id(1) == 0)
    def _():
        state_ref[...] = jnp.zeros(state_ref.shape, F32)

    q = qk_ref[:, :RET_HEADS * RET_QK_DIM]
    k = qk_ref[:, RET_HEADS * RET_QK_DIM:]
    ii = lax.broadcasted_iota(jnp.int32, (c, c), 0)
    jj = lax.broadcasted_iota(jnp.int32, (c, c), 1)
    diff = (ii - jj).astype(F32)
    row = lax.broadcasted_iota(jnp.int32, (c, 1), 0).astype(F32)
    lane = lax.broadcasted_iota(jnp.int32, (c, RET_HEADS * RET_QK_DIM), 1)
    for h in range(RET_HEADS):
        log_g = math.log(1.0 - 2.0 ** (-5.0 - h))
        intra = jnp.where(diff >= 0, jnp.exp(jnp.maximum(diff, 0.0) * log_g), 0.0)
        q_dec = jnp.exp((row + 1.0) * log_g)
        k_dec = jnp.exp((c - 1.0 - row) * log_g)
        chunk_dec = math.exp(c * log_g)
        qm = jnp.where((lane // RET_QK_DIM) == h, q, jnp.zeros((), BF16))
        vh = v_ref[:, h * RET_V_DIM:(h + 1) * RET_V_DIM]
        att = _dot_nt(qm, k) * intra
        st = state_ref[h]
        o = _dot(att.astype(BF16), vh) + _dot(qm, st.astype(BF16)) * q_dec
        kd = (k.astype(F32) * k_dec).T.astype(BF16)
        state_ref[h] = st * chunk_dec + _dot(kd, vh)
        mu = jnp.mean(o, axis=-1, keepdims=True)
        var = jnp.mean(jnp.square(o - mu), axis=-1, keepdims=True)
        o = (o - mu) * lax.rsqrt(var + NORM_EPS)
        zh = z_ref[:, h * RET_V_DIM:(h + 1) * RET_V_DIM].astype(F32)
        o_ref[:, h * RET_V_DIM:(h + 1) * RET_V_DIM] = (o * _silu(zh)).astype(BF16)


def _retention(slab, batch, seq):
    nc = seq // RET_CHUNK
    w = RET_HEADS * RET_V_DIM
    return pl.pallas_call(
        _ret_kernel,
        grid=(batch, nc),
        in_specs=[
            pl.BlockSpec((RET_CHUNK, w), lambda b, c: (b * nc + c, _cb('ret_q', w))),
            pl.BlockSpec((RET_CHUNK, w), lambda b, c: (b * nc + c, _cb('ret_v', w))),
            pl.BlockSpec((RET_CHUNK, w), lambda b, c: (b * nc + c, _cb('ret_z', w))),
        ],
        out_specs=pl.BlockSpec((RET_CHUNK, w), lambda b, c: (b * nc + c, 0)),
        out_shape=jax.ShapeDtypeStruct((batch * seq, w), BF16),
        scratch_shapes=[pltpu.VMEM((RET_HEADS, RET_HEADS * RET_QK_DIM, RET_V_DIM), F32)],
        compiler_params=pltpu.CompilerParams(dimension_semantics=("parallel", "arbitrary")),
        name="retention",
    )(slab, slab, slab)


def _compress_kernel(r_ref, w1_ref, w2_ref, pos_ref, cos_ref, sa_ref, sb_ref, o_ref):
    kind = pl.program_id(0) // NSA_KV_HEADS
    half = NSA_CMP_STRIDE * HEAD_DIM
    r = r_ref[...]
    n_rows = r.shape[0]
    lo = _dot(r, w1_ref[:half, :])
    hi = _dot(r, w1_ref[half:, :])
    pos = _dot(pos_ref[...], w1_ref[...])[0:1, :]
    pre = lo + pltpu.roll(hi, n_rows - 1, 0) + pos
    y = _dot(jax.nn.gelu(pre).astype(BF16), w2_ref[...])
    roped = _rope_chunk(y, cos_ref[...], sa_ref[...], sb_ref[...])
    o_ref[...] = jnp.where(kind == 0, roped, y)


def _compress(r, w1, w2, pos8, cos_c, sa_c, sb_c, batch):
    n_rows = r.shape[2]
    flat = NSA_CMP_LEN * HEAD_DIM
    return pl.pallas_call(
        _compress_kernel,
        grid=(2 * NSA_KV_HEADS, batch),
        in_specs=[
            pl.BlockSpec((None, None, n_rows, flat // 2), lambda i, b: (i, b, 0, 0)),
            pl.BlockSpec((None, flat, NSA_CMP_HIDDEN), lambda i, b: (i // NSA_KV_HEADS, 0, 0)),
            pl.BlockSpec((None, NSA_CMP_HIDDEN, LANES), lambda i, b: (i // NSA_KV_HEADS, 0, 0)),
            pl.BlockSpec((8, flat), lambda i, b: (0, 0)),
            pl.BlockSpec((n_rows, LANES), lambda i, b: (0, 0)),
            pl.BlockSpec((n_rows, LANES), lambda i, b: (0, 0)),
            pl.BlockSpec((n_rows, LANES), lambda i, b: (0, 0)),
        ],
        out_specs=pl.BlockSpec((None, None, n_rows, LANES), lambda i, b: (i, b, 0, 0)),
        out_shape=jax.ShapeDtypeStruct((2 * NSA_KV_HEADS, batch, n_rows, LANES), F32),
        compiler_params=pltpu.CompilerParams(dimension_semantics=("parallel", "parallel")),
        name="nsa_compress",
    )(r, w1, w2, pos8, cos_c, sa_c, sb_c)


NSA_WIN_SPAN = TQ + NSA_WINDOW
SEL_LANE0 = 64


def _nsa_kernel(q_ref, ks_ref, vs_ref, kw_ref, vw_ref, kc_ref, vc_ref, g_ref, z_ref, o_ref,
                ksa_ref, ksb_ref, vsa_ref, vsb_ref, kwr_ref, vwa_ref, vwb_ref, kcr_ref, vcr_ref,
                ocmp_ref, bias_ref, *, seq):
    hk = pl.program_id(1)
    t = pl.program_id(2)
    n_slc = seq // NSA_SLC_BLOCK
    n_cmp = (seq - NSA_CMP_LEN) // NSA_CMP_STRIDE + 1
    n_crow = kc_ref.shape[0]
    w = GQA_W

    @pl.when(t == 0)
    def _():
        ks = _rep2(ks_ref[...].astype(F32), hk).astype(BF16)
        rowblk = lax.broadcasted_iota(jnp.int32, (seq, LANES), 0) // NSA_SLC_BLOCK
        lane = lax.broadcasted_iota(jnp.int32, (seq, LANES), 1)
        onehot = jnp.where(lane - SEL_LANE0 == rowblk, 1.0, 0.0).astype(BF16)
        ksa_ref[...] = jnp.concatenate([ks, onehot], axis=1)
        ksb_ref[...] = jnp.concatenate([onehot, ks], axis=1)
        vsa_ref[...], vsb_ref[...] = _value_variants(_rep2(vs_ref[...].astype(F32), hk))
        kwin = _rep2(kw_ref[...].astype(F32), hk).astype(BF16)
        kwr_ref[...] = jnp.concatenate([kwin, kwin], axis=1)
        vwa_ref[...], vwb_ref[...] = _value_variants(_rep2(vw_ref[...].astype(F32), hk))
        kc = kc_ref[...]
        kc = (kc + pltpu.roll(kc, HEAD_DIM, 1)).astype(BF16)
        kcr_ref[...] = jnp.concatenate([kc, kc], axis=1)
        vc = vc_ref[...]
        vc = (vc + pltpu.roll(vc, HEAD_DIM, 1)).astype(BF16)
        vcr_ref[...] = jnp.concatenate([vc, vc], axis=1)

        q_all = q_ref[...]
        pos_all = lax.broadcasted_iota(jnp.int32, (seq, 1), 0)
        ci = lax.broadcasted_iota(jnp.int32, (1, n_crow), 1)
        cmask = jnp.logical_and(ci * NSA_CMP_STRIDE + NSA_CMP_LEN - 1 <= pos_all, ci < n_cmp)
        kcr, vcr = kcr_ref[...], vcr_ref[...]
        p_sum = jnp.zeros((seq, n_crow), F32)
        o_cmp = jnp.zeros((seq, w), F32)
        for g in range(4):
            mask_g = _head_mask(g, w, seq)
            s = jnp.where(cmask, _dot_nt(jnp.where(mask_g, q_all, jnp.zeros((), BF16)), kcr), NEG)
            m = jnp.max(s, axis=-1, keepdims=True)
            e = jnp.where(cmask, jnp.exp2(s - m), 0.0)
            p = e * (1.0 / jnp.maximum(jnp.sum(e, axis=-1, keepdims=True), TINY))
            p_sum = p_sum + p
            o_cmp = jnp.where(mask_g, _dot(p.astype(BF16), vcr), o_cmp)
        ocmp_ref[...] = o_cmp

        c0 = lax.broadcasted_iota(jnp.int32, (n_slc, n_crow), 1) * NSA_CMP_STRIDE
        s0 = lax.broadcasted_iota(jnp.int32, (n_slc, n_crow), 0) * NSA_SLC_BLOCK
        ov = jnp.minimum(c0 + NSA_CMP_LEN, s0 + NSA_SLC_BLOCK) - jnp.maximum(c0, s0)
        ov = (jnp.maximum(ov, 0).astype(F32) * (1.0 / NSA_CMP_LEN)).astype(BF16)
        imp = _dot_nt(jnp.concatenate([ov, ov, ov], axis=1),
                      jnp.concatenate(_split3(p_sum), axis=1).astype(BF16))
        j_iota = lax.broadcasted_iota(jnp.int32, (n_slc, seq), 0)
        q_blk = lax.broadcasted_iota(jnp.int32, (n_slc, seq), 1) // NSA_SLC_BLOCK
        forced = jnp.logical_or(j_iota == 0, jnp.logical_or(j_iota == q_blk, j_iota == q_blk - 1))
        imp = jnp.where(forced, jnp.inf, imp)
        imp = jnp.where(j_iota <= q_blk, imp, -jnp.inf)
        cnt = _rank_before(imp, n_slc)
        allowed = jnp.logical_and(cnt < min(NSA_SLC_TOPN, n_slc), j_iota <= q_blk)
        bias_ref[...] = _bias_lanes(jnp.where(allowed, 1.0, 0.0), SEL_LANE0)

    rows = pl.ds(pl.multiple_of(t * TQ, TQ), TQ)
    q = q_ref[rows, :]
    bias = bias_ref[rows, :]
    zero = jnp.zeros((), BF16)
    masks = [_head_mask(g, w) for g in range(4)]
    qms = [jnp.where(masks[g], q, zero) for g in range(4)]
    qaugs = []
    for g in range(4):
        half_mask = _head_mask(g % 2, LANES)
        if g < 2:
            qaugs.append(jnp.concatenate([jnp.where(half_mask, q[:, :LANES], zero), bias], axis=1))
        else:
            qaugs.append(jnp.concatenate([bias, jnp.where(half_mask, q[:, LANES:], zero)], axis=1))

    gates = jax.nn.sigmoid(g_ref[...].astype(F32))
    g_hi = gates.astype(BF16)
    g_lo = (gates - g_hi.astype(F32)).astype(BF16)
    er = lax.broadcasted_iota(jnp.int32, (2 * LANES, 3 * w), 0) % LANES
    ec = lax.broadcasted_iota(jnp.int32, (2 * LANES, 3 * w), 1)
    expand = jnp.where(er == (ec // w) * NSA_HEADS + hk * 4 + (ec % w) // HEAD_DIM, 1.0, 0.0).astype(BF16)
    gx = _dot(jnp.concatenate([g_hi, g_lo], axis=1), expand)
    o_cmp = gx[:, :w] * ocmp_ref[rows, :]
    gate_z = _silu(z_ref[...].astype(F32))

    for tt in range(seq // TQ):
        @pl.when(t == tt)
        def _(tt=tt):
            k0 = max(tt * TQ - NSA_WINDOW, 0)
            kwin = kwr_ref[k0:k0 + NSA_WIN_SPAN, :]
            dist = (tt * TQ + lax.broadcasted_iota(jnp.int32, (TQ, 1), 0)) - (
                k0 + lax.broadcasted_iota(jnp.int32, (1, NSA_WIN_SPAN), 1))
            wmask = jnp.logical_and(dist >= 0, dist < NSA_WINDOW)
            o_win = jnp.zeros((TQ, w), F32)
            o_slc = jnp.zeros((TQ, w), F32)
            for g in range(4):
                vref, ones = (vwa_ref, ONES_A) if g < 3 else (vwb_ref, ONES_B)
                s = jnp.where(wmask, _dot_nt(qms[g], kwin), NEG)
                m = jnp.max(s, axis=-1, keepdims=True)
                o = _dot(jnp.exp2(s - m).astype(BF16), vref[k0:k0 + NSA_WIN_SPAN, :])
                o_win = jnp.where(masks[g], _normalize(o, ones), o_win)

                kref = ksa_ref if g < 2 else ksb_ref
                vref, ones = (vsa_ref, ONES_A) if g < 3 else (vsb_ref, ONES_B)
                o = _prefix_attention(
                    qaugs[g],
                    lambda start, size, kref=kref: kref[start:start + size, :],
                    lambda start, size, vref=vref: vref[start:start + size, :],
                    tt + 1, ones)
                o_slc = jnp.where(masks[g], o, o_slc)
            o = o_cmp + gx[:, w:2 * w] * o_slc + gx[:, 2 * w:] * o_win
            o_ref[...] = (o * gate_z).astype(BF16)


def _nsa(slab, cmp_kv, batch, seq):
    nt = seq // TQ
    w = GQA_W
    n_crow = cmp_kv.shape[2]
    kv_spec = lambda name: pl.BlockSpec((seq, LANES), lambda b, h, t: (b, _cb(name)))
    big = pltpu.VMEM((seq, w), BF16)
    return pl.pallas_call(
        functools.partial(_nsa_kernel, seq=seq),
        grid=(batch, NSA_KV_HEADS, nt),
        in_specs=[
            pl.BlockSpec((seq, w), lambda b, h, t: (b, _cb('nsa_q', w) + h)),
            kv_spec('nsa_ks'), kv_spec('nsa_vs'), kv_spec('nsa_kw'), kv_spec('nsa_vw'),
            pl.BlockSpec((None, None, n_crow, LANES), lambda b, h, t: (h, b, 0, 0)),
            pl.BlockSpec((None, None, n_crow, LANES), lambda b, h, t: (NSA_KV_HEADS + h, b, 0, 0)),
            pl.BlockSpec((TQ, LANES), lambda b, h, t: (b * nt + t, _cb('nsa_g'))),
            pl.BlockSpec((TQ, w), lambda b, h, t: (b * nt + t, _cb('nsa_z', w) + h)),
        ],
        out_specs=pl.BlockSpec((TQ, w), lambda b, h, t: (b * nt + t, h)),
        out_shape=jax.ShapeDtypeStruct((batch * seq, NSA_HEADS * HEAD_DIM), BF16),
        scratch_shapes=[
            big, big, big, big, big, big, big,
            pltpu.VMEM((n_crow, w), BF16), pltpu.VMEM((n_crow, w), BF16),
            pltpu.VMEM((seq, w), F32), pltpu.VMEM((seq, LANES), BF16),
        ],
        compiler_params=pltpu.CompilerParams(dimension_semantics=("parallel", "parallel", "arbitrary")),
        name="nsa",
    )(slab, slab, slab, slab, slab, cmp_kv, cmp_kv, slab, slab)


def _merge_kernel(x_ref, g_ref, oa_ref, ob_ref, oc_ref, od_ref, wb_ref, wo_ref, fg_ref, o_ref, *, final):
    merged = None
    row = 0
    for i, br in enumerate((oa_ref, ob_ref, oc_ref, od_ref)):
        width = br.shape[1]
        y = _dot(br[...], wb_ref[row:row + width, :])
        gate = jax.nn.sigmoid(g_ref[:, i * D_MODEL:(i + 1) * D_MODEL].astype(F32))
        merged = gate * y if merged is None else merged + gate * y
        row += width
    x = x_ref[...] + _dot(merged.astype(BF16), wo_ref[...])
    if final:
        ms = jnp.mean(x * x, axis=-1, keepdims=True)
        x = x * lax.rsqrt(ms + NORM_EPS) * fg_ref[...]
    o_ref[...] = x


def _merge(x2, slab, outs, w_branch, w_out, final_g, final):
    t = x2.shape[0]
    tm = MERGE_TM
    d_branch = w_branch.shape[0]
    row_spec = lambda width: pl.BlockSpec((tm, width), lambda i: (i, 0))
    return pl.pallas_call(
        functools.partial(_merge_kernel, final=final),
        grid=(t // tm,),
        in_specs=[
            row_spec(D_MODEL),
            pl.BlockSpec((tm, N_BRANCH * D_MODEL), lambda i: (i, _cb('merge_g', N_BRANCH * D_MODEL))),
            row_spec(outs[0].shape[1]), row_spec(outs[1].shape[1]),
            row_spec(outs[2].shape[1]), row_spec(outs[3].shape[1]),
            pl.BlockSpec((d_branch, D_MODEL), lambda i: (0, 0)),
            pl.BlockSpec((D_MODEL, D_MODEL), lambda i: (0, 0)),
            pl.BlockSpec((1, D_MODEL), lambda i: (0, 0)),
        ],
        out_specs=row_spec(D_MODEL),
        out_shape=jax.ShapeDtypeStruct((t, D_MODEL), F32),
        compiler_params=pltpu.CompilerParams(dimension_semantics=("parallel",)),
        name="merge_out",
    )(x2, slab, *outs, w_branch, w_out, final_g.reshape(1, D_MODEL))


def _rope_tables(positions):
    inv = 1.0 / (ROPE_THETA ** (jnp.arange(0, HEAD_DIM, 2, dtype=F32) / HEAD_DIM))
    ang = positions.astype(F32)[:, None] * inv[None, :]
    cos, sin = jnp.cos(ang), jnp.sin(ang)
    zero = jnp.zeros_like(sin)
    cos_t = jnp.tile(cos, (1, 4))
    sa_t = jnp.tile(jnp.concatenate([-sin, zero], axis=1), (1, 2))
    sb_t = jnp.tile(jnp.concatenate([zero, sin], axis=1), (1, 2))
    return cos_t, sa_t, sb_t


def kernel(x, norm_g, w_in, w_branch, w_out, swa_sink, nsa_cmp_pos, nsa_w_ck1, nsa_w_ck2,
           nsa_w_cv1, nsa_w_cv2, final_norm_g):
    batch, seq, d = x.shape
    depth = w_in.shape[0]
    assert d == D_MODEL and seq % TQ == 0 and seq % RET_CHUNK == 0 and (batch * seq) % MERGE_TM == 0
    assert seq >= NSA_WIN_SPAN and seq % (8 * NSA_CMP_STRIDE) == 0 and (seq // MOBA_BLOCK) % 8 == 0

    cos_t, sa_t, sb_t = _rope_tables(jnp.arange(seq))
    n_crow = seq // NSA_CMP_STRIDE
    cos_c, sa_c, sb_c = _rope_tables(jnp.arange(n_crow) * NSA_CMP_STRIDE + NSA_CMP_LEN - 1)

    x2 = x.reshape(batch * seq, d)
    for l in range(depth):
        slab, kcvc = _proj_in(x2, norm_g[l], _slab_weights(w_in[l]), cos_t, sa_t, sb_t, seq)

        groups = kcvc.reshape(2 * NSA_KV_HEADS, batch, n_crow, NSA_CMP_STRIDE * HEAD_DIM)
        w1 = jnp.stack([nsa_w_ck1[l], nsa_w_cv1[l]]).astype(BF16)
        w2 = jnp.pad(jnp.stack([nsa_w_ck2[l], nsa_w_cv2[l]]), ((0, 0), (0, 0), (0, LANES - HEAD_DIM))).astype(BF16)
        pos8 = jnp.broadcast_to(nsa_cmp_pos[l].reshape(1, -1), (8, NSA_CMP_LEN * HEAD_DIM)).astype(BF16)
        cmp_kv = _compress(groups, w1, w2, pos8, cos_c, sa_c, sb_c, batch)

        outs = (
            _moba(slab, batch, seq),
            _swa(slab, swa_sink[l], batch, seq),
            _retention(slab, batch, seq),
            _nsa(slab, cmp_kv, batch, seq),
        )
        x2 = _merge(x2, slab, outs, w_branch[l].astype(BF16), w_out[l].astype(BF16),
                    final_norm_g, final=(l == depth - 1))
    return x2.reshape(batch, seq, d)
```

```python
import functools
import math

import numpy as np
import jax
import jax.numpy as jnp
from jax import lax
from jax.experimental import pallas as pl
from jax.experimental.pallas import tpu as pltpu

F32 = jnp.float32
BF16 = jnp.bfloat16

D_MODEL = 1024
HEAD_DIM = 64
ROPE_THETA = 10000.0
NORM_EPS = 1e-6
TINY = 1e-30
N_BRANCH = 4
NEG = -1e30
LOG2E = 1.4426950408889634

MOBA_HEADS = 8
MOBA_BLOCK = 256
MOBA_TOPK = 3

SWA_HEADS = 8
SWA_KV_HEADS = 2
SWA_WINDOW = 128

RET_HEADS = 4
RET_QK_DIM = 64
RET_V_DIM = 128

NSA_HEADS = 8
NSA_KV_HEADS = 2
NSA_CMP_LEN = 32
NSA_CMP_STRIDE = 16
NSA_CMP_HIDDEN = 256
NSA_SLC_BLOCK = 64
NSA_SLC_TOPN = 16
NSA_WINDOW = 512

LANES = 128
MXU_N = 256
TQ = 256
MOBA_PAIRS_PER_ITER = 2
PROJ_TM = 2048
PROJ_TN = 1024
MERGE_TM = 512
RET_CHUNK = 256

_IN_SPLITS = (
    ('moba_q', 512), ('moba_k', 512), ('moba_v', 512), ('moba_z', 512),
    ('swa_q', 512), ('swa_k', 128), ('swa_v', 128), ('swa_z', 512),
    ('ret_q', 256), ('ret_k', 256), ('ret_v', 512), ('ret_z', 512),
    ('nsa_q', 512), ('nsa_kc', 128), ('nsa_vc', 128),
    ('nsa_ks', 128), ('nsa_vs', 128), ('nsa_kw', 128), ('nsa_vw', 128),
    ('nsa_g', 24), ('nsa_z', 512),
    ('merge_g', 4096),
)
_WIDTH = dict(_IN_SPLITS)
_SLAB = (
    ('merge_g', 4096),
    ('moba_q', 512), ('moba_k', 512), ('swa_q', 512), ('nsa_q', 512),
    ('ret_q', 256), ('ret_k', 256),
    ('swa_k', 128), ('nsa_ks', 128), ('nsa_kw', 128), (None, 128),
    ('moba_v', 512), ('moba_z', 512), ('swa_z', 512), ('ret_v', 512), ('ret_z', 512), ('nsa_z', 512),
    ('swa_v', 128), ('nsa_vs', 128), ('nsa_vw', 128), ('nsa_g', 128),
    ('nsa_kc', 128), ('nsa_vc', 128), (None, 256),
)
_COL_SCALE = {'moba_q': 0.125 * LOG2E, 'swa_q': 0.125 * LOG2E, 'nsa_q': 0.125 * LOG2E, 'ret_k': 0.125}


def _slab_layout():
    src_off, off = {}, 0
    for name, w in _IN_SPLITS:
        src_off[name] = off
        off += w
    col, pos = {}, 0
    for name, w in _SLAB:
        if name is not None:
            col[name] = pos
        pos += w
    return src_off, col, pos


_SRC_OFF, COL, D_SLAB = _slab_layout()


def _slab_weights(w):
    parts = []
    for name, width in _SLAB:
        if name is None:
            parts.append(jnp.zeros((w.shape[0], width), BF16))
            continue
        blk = w[:, _SRC_OFF[name]:_SRC_OFF[name] + _WIDTH[name]]
        if name in _COL_SCALE:
            blk = blk * _COL_SCALE[name]
        parts.append(jnp.pad(blk.astype(BF16), ((0, 0), (0, width - _WIDTH[name]))))
    return jnp.concatenate(parts, axis=1)
ROPE_COL_LO, ROPE_COL_HI = COL['moba_q'], COL['moba_v']
assert ROPE_COL_LO % PROJ_TN == 0 and ROPE_COL_HI % PROJ_TN == 0 and D_SLAB % PROJ_TN == 0
KCVC_TILE, KCVC_OFF = divmod(COL['nsa_kc'], PROJ_TN)


def _cb(name, width=LANES):
    assert COL[name] % width == 0
    return COL[name] // width


NT = (((1,), (1,)), ((), ()))


def _dot(a, b):
    return jnp.dot(a, b, preferred_element_type=F32)


def _dot_nt(a, b):
    return lax.dot_general(a, b, NT, preferred_element_type=F32)


def _silu(z):
    return z * jax.nn.sigmoid(z)


def _rope_chunk(y, cos, sin_a, sin_b):
    return y * cos + pltpu.roll(y, 96, 1) * sin_a + pltpu.roll(y, 32, 1) * sin_b


def _split3(x):
    x1 = x.astype(BF16).astype(F32)
    x2 = (x - x1).astype(BF16).astype(F32)
    x3 = (x - x1 - x2).astype(BF16).astype(F32)
    return x1, x2, x3


def _proj_kernel(x_ref, g_ref, w_ref, cos_ref, sa_ref, sb_ref, o_ref, kcvc_ref, h_ref):
    j = pl.program_id(1)

    @pl.when(j == 0)
    def _():
        x = x_ref[...]
        ms = jnp.mean(x * x, axis=-1, keepdims=True)
        h_ref[...] = (x * lax.rsqrt(ms + NORM_EPS) * g_ref[...]).astype(BF16)

    is_rope = jnp.logical_and(j >= ROPE_COL_LO // PROJ_TN, j < ROPE_COL_HI // PROJ_TN)

    @pl.when(is_rope)
    def _():
        cos, sa, sb = cos_ref[...], sa_ref[...], sb_ref[...]
        for c in range(PROJ_TN // MXU_N):
            y = _dot(h_ref[...], w_ref[:, c * MXU_N:(c + 1) * MXU_N])
            for half in range(MXU_N // LANES):
                lo = c * MXU_N + half * LANES
                o_ref[:, lo:lo + LANES] = _rope_chunk(y[:, half * LANES:(half + 1) * LANES], cos, sa, sb).astype(BF16)

    @pl.when(jnp.logical_not(is_rope))
    def _():
        y = _dot(h_ref[...], w_ref[...])
        o_ref[...] = y.astype(BF16)

        @pl.when(j == KCVC_TILE)
        def _():
            for c in range(2 * NSA_KV_HEADS):
                lo = KCVC_OFF + c * HEAD_DIM
                kcvc_ref[c] = y[:, lo:lo + HEAD_DIM].astype(BF16)


def _proj_in(x2, norm_g, w_slab, cos_t, sa_t, sb_t, seq):
    t = x2.shape[0]
    tm = min(PROJ_TM, seq)
    per_seq = seq // tm
    return pl.pallas_call(
        _proj_kernel,
        grid=(t // tm, D_SLAB // PROJ_TN),
        in_specs=[
            pl.BlockSpec((tm, D_MODEL), lambda i, j: (i, 0)),
            pl.BlockSpec((1, D_MODEL), lambda i, j: (0, 0)),
            pl.BlockSpec((D_MODEL, PROJ_TN), lambda i, j: (0, j)),
            pl.BlockSpec((tm, LANES), lambda i, j: (i % per_seq, 0)),
            pl.BlockSpec((tm, LANES), lambda i, j: (i % per_seq, 0)),
            pl.BlockSpec((tm, LANES), lambda i, j: (i % per_seq, 0)),
        ],
        out_specs=[
            pl.BlockSpec((tm, PROJ_TN), lambda i, j: (i, j)),
            pl.BlockSpec((2 * NSA_KV_HEADS, tm, HEAD_DIM), lambda i, j: (0, i, 0)),
        ],
        out_shape=[
            jax.ShapeDtypeStruct((t, D_SLAB), BF16),
            jax.ShapeDtypeStruct((2 * NSA_KV_HEADS, t, HEAD_DIM), BF16),
        ],
        scratch_shapes=[pltpu.VMEM((tm, D_MODEL), BF16)],
        compiler_params=pltpu.CompilerParams(dimension_semantics=("parallel", "arbitrary")),
        name="proj_in",
    )(x2, norm_g.reshape(1, D_MODEL), w_slab, cos_t, sa_t, sb_t)


def _rank_before(scores, n):
    j_iota = lax.broadcasted_iota(jnp.int32, scores.shape, 0)
    cnt = jnp.zeros(scores.shape, F32)
    for jp in range(n):
        r = scores[jp:jp + 1, :]
        ahead = jnp.logical_or(r > scores, jnp.logical_and(r == scores, jp < j_iota))
        cnt = cnt + jnp.where(ahead, 1.0, 0.0)
    return cnt


def _normalize(o, ones_lane):
    return o * (1.0 / jnp.maximum(o[:, ones_lane:ones_lane + 1], TINY))


def _prefix_attention(qaug, k_at, v_at, n_tiles, ones_lane):
    n_past = (n_tiles - 1) * TQ
    row = lax.broadcasted_iota(jnp.int32, (TQ, TQ), 0)
    col = lax.broadcasted_iota(jnp.int32, (TQ, TQ), 1)
    s_d = jnp.where(col <= row, _dot_nt(qaug, k_at(n_past, TQ)), NEG)
    m = jnp.max(s_d, axis=-1, keepdims=True)
    if n_past:
        s_p = _dot_nt(qaug, k_at(0, n_past))
        m = jnp.maximum(m, jnp.max(s_p, axis=-1, keepdims=True))
    o = _dot(jnp.exp2(s_d - m).astype(BF16), v_at(n_past, TQ))
    if n_past:
        o = o + _dot(jnp.exp2(s_p - m).astype(BF16), v_at(0, n_past))
    return _normalize(o, ones_lane)


def _bias_lanes(allowed_t, row0):
    n, rows = allowed_t.shape
    pieces = []
    if row0:
        pieces.append(jnp.zeros((row0, rows), F32))
    pieces.append(allowed_t)
    if LANES - row0 - n:
        pieces.append(jnp.zeros((LANES - row0 - n, rows), F32))
    full = jnp.concatenate(pieces, axis=0)
    return ((full.T - 1.0) * (-NEG)).astype(BF16)


def _moba_kernel(q_ref, k_ref, v_ref, z_ref, o_ref, qp_ref, kaug_ref, vaug_ref, bias_ref, osc_ref, *, seq):
    t = pl.program_id(1)
    nb = seq // MOBA_BLOCK
    n_pair = MOBA_HEADS // 2

    @pl.when(t == 0)
    def _():
        for p in range(n_pair):
            sl = slice(p * LANES, (p + 1) * LANES)
            qp_ref[p] = q_ref[:, sl]
            kaug_ref[2 * p] = k_ref[:, sl]
            vaug_ref[2 * p] = v_ref[:, sl]
        rowblk = lax.broadcasted_iota(jnp.int32, (seq, LANES), 0) // MOBA_BLOCK
        lane = lax.broadcasted_iota(jnp.int32, (seq, LANES), 1)
        low = lane < HEAD_DIM
        blk = lax.broadcasted_iota(jnp.int32, (nb, seq), 0)
        own = lax.broadcasted_iota(jnp.int32, (nb, seq), 1) // MOBA_BLOCK
        avg = jnp.where(blk == own, 1.0 / MOBA_BLOCK, 0.0).astype(BF16)

        def build(p, carry):
            k = kaug_ref[2 * p]
            kf = k.astype(F32)
            vf = vaug_ref[2 * p].astype(F32)
            kaug_ref[2 * p] = jnp.where(low, kf, jnp.where(lane - HEAD_DIM == rowblk, 1.0, 0.0)).astype(BF16)
            kaug_ref[2 * p + 1] = jnp.where(low, jnp.where(lane == rowblk, 1.0, 0.0), kf).astype(BF16)
            vaug_ref[2 * p] = jnp.where(low, vf, 1.0).astype(BF16)
            vaug_ref[2 * p + 1] = jnp.where(low, 1.0, vf).astype(BF16)
            km3 = jnp.concatenate(_split3(_dot(avg, k)), axis=0).astype(BF16)
            q_all = qp_ref[p]
            for h in range(2):
                mine = low if h == 0 else jnp.logical_not(low)
                g3 = _dot_nt(km3, jnp.where(mine, q_all, jnp.zeros((), BF16)))
                gate = g3[:nb] + g3[nb:2 * nb] + g3[2 * nb:]
                gate = jnp.where(blk < own, gate, -jnp.inf)
                cnt = _rank_before(gate, nb)
                allowed = jnp.logical_or(jnp.logical_and(cnt < MOBA_TOPK, blk < own), blk == own)
                bias_ref[2 * p + h] = _bias_lanes(jnp.where(allowed, 1.0, 0.0), HEAD_DIM if h == 0 else 0)
            return carry

        lax.fori_loop(0, n_pair, build, 0)

    rows = pl.ds(pl.multiple_of(t * TQ, TQ), TQ)
    lane = lax.broadcasted_iota(jnp.int32, (TQ, LANES), 1)

    for tt in range(nb):
        @pl.when(t == tt)
        def _(tt=tt):
            def body(it, carry):
                for pp in range(MOBA_PAIRS_PER_ITER):
                    p = it * MOBA_PAIRS_PER_ITER + pp
                    q = qp_ref[p, rows, :]
                    outs = []
                    for h in range(2):
                        mine = (lane < HEAD_DIM) if h == 0 else (lane >= HEAD_DIM)
                        i = 2 * p + h
                        outs.append(_prefix_attention(
                            jnp.where(mine, q, bias_ref[i, rows, :]),
                            lambda start, size, i=i: kaug_ref[i, pl.ds(start, size), :],
                            lambda start, size, i=i: vaug_ref[i, pl.ds(start, size), :],
                            tt + 1, HEAD_DIM if h == 0 else 0))
                    osc_ref[p] = jnp.where(lane < HEAD_DIM, outs[0], outs[1])
                return carry

            lax.fori_loop(0, n_pair // MOBA_PAIRS_PER_ITER, body, 0)

    gate_z = _silu(z_ref[...].astype(F32))
    for p in range(n_pair):
        sl = slice(p * LANES, (p + 1) * LANES)
        o_ref[:, sl] = (osc_ref[p] * gate_z[:, sl]).astype(BF16)


def _moba(slab, batch, seq):
    nt = seq // TQ
    w = MOBA_HEADS * HEAD_DIM
    n_pair = MOBA_HEADS // 2
    plane = lambda n: pltpu.VMEM((n, seq, LANES), BF16)
    return pl.pallas_call(
        functools.partial(_moba_kernel, seq=seq),
        grid=(batch, nt),
        in_specs=[
            pl.BlockSpec((seq, w), lambda b, t: (b, _cb('moba_q', w))),
            pl.BlockSpec((seq, w), lambda b, t: (b, _cb('moba_k', w))),
            pl.BlockSpec((seq, w), lambda b, t: (b, _cb('moba_v', w))),
            pl.BlockSpec((TQ, w), lambda b, t: (b * nt + t, _cb('moba_z', w))),
        ],
        out_specs=pl.BlockSpec((TQ, w), lambda b, t: (b * nt + t, 0)),
        out_shape=jax.ShapeDtypeStruct((batch * seq, w), BF16),
        scratch_shapes=[
            plane(n_pair), plane(MOBA_HEADS), plane(MOBA_HEADS), plane(MOBA_HEADS),
            pltpu.VMEM((n_pair, TQ, LANES), F32),
        ],
        compiler_params=pltpu.CompilerParams(dimension_semantics=("parallel", "arbitrary")),
        name="moba",
    )(slab, slab, slab, slab)


GQA_W = 4 * HEAD_DIM
ONES_A, ONES_B = 3 * HEAD_DIM, 0


def _rep2(x, hk):
    lane = lax.broadcasted_iota(jnp.int32, x.shape, 1)
    keep = (lane < HEAD_DIM) == (hk == 0)
    return jnp.where(keep, x, pltpu.roll(x, HEAD_DIM, 1))


def _value_variants(v2):
    low = lax.broadcasted_iota(jnp.int32, v2.shape, 1) < HEAD_DIM
    va = jnp.concatenate([v2, jnp.where(low, v2, 1.0)], axis=1).astype(BF16)
    vb = jnp.concatenate([jnp.where(low, 1.0, v2), v2], axis=1).astype(BF16)
    return va, vb


def _head_mask(g, width, rows=TQ):
    lane = lax.broadcasted_iota(jnp.int32, (rows, width), 1)
    return (lane // HEAD_DIM) == g


SWA_SPAN = TQ + LANES
SWA_TILES_PER_STEP = 2


def _swa_kernel(sink_ref, q_ref, k_ref, v_ref, z_ref, o_ref, krep_ref, va_ref, vb_ref):
    t = pl.program_id(1)

    @pl.when(t == 0)
    def _():
        for hk in range(SWA_KV_HEADS):
            kr = _rep2(k_ref[...].astype(F32), hk).astype(BF16)
            krep_ref[hk] = jnp.concatenate([kr, kr], axis=1)
            va_ref[hk], vb_ref[hk] = _value_variants(_rep2(v_ref[...].astype(F32), hk))

    gate_z = _silu(z_ref[...].astype(F32))
    masks = [_head_mask(g, GQA_W) for g in range(4)]
    for sub in range(SWA_TILES_PER_STEP):
        r0 = (t * SWA_TILES_PER_STEP + sub) * TQ
        rows = slice(sub * TQ, (sub + 1) * TQ)
        k0 = pl.multiple_of(jnp.maximum(r0 - LANES, 0), LANES)
        dist = (r0 + lax.broadcasted_iota(jnp.int32, (TQ, SWA_SPAN), 0)) - (
            k0 + lax.broadcasted_iota(jnp.int32, (TQ, SWA_SPAN), 1))
        mask = jnp.logical_and(dist >= 0, dist < SWA_WINDOW)
        for hk in range(SWA_KV_HEADS):
            cols = slice(hk * GQA_W, (hk + 1) * GQA_W)
            q = q_ref[rows, cols]
            kw = krep_ref[hk, pl.ds(k0, SWA_SPAN), :]
            out = jnp.zeros((TQ, GQA_W), F32)
            for g in range(4):
                vref, ones = (va_ref, ONES_A) if g < 3 else (vb_ref, ONES_B)
                s = jnp.where(mask, _dot_nt(jnp.where(masks[g], q, jnp.zeros((), BF16)), kw), NEG)
                sink = sink_ref[hk * 4 + g] * LOG2E
                m = jnp.maximum(jnp.max(s, axis=-1, keepdims=True), sink)
                o = _dot(jnp.exp2(s - m).astype(BF16), vref[hk, pl.ds(k0, SWA_SPAN), :])
                den = o[:, ones:ones + 1] + jnp.exp2(sink - m)
                out = jnp.where(masks[g], o * (1.0 / jnp.maximum(den, TINY)), out)
            o_ref[rows, cols] = (out * gate_z[rows, cols]).astype(BF16)


def _swa(slab, sink, batch, seq):
    rows = TQ * SWA_TILES_PER_STEP
    ns = seq // rows
    w = SWA_HEADS * HEAD_DIM
    plane = pltpu.VMEM((SWA_KV_HEADS, seq, GQA_W), BF16)
    return pl.pallas_call(
        _swa_kernel,
        grid=(batch, ns),
        in_specs=[
            pl.BlockSpec(memory_space=pltpu.SMEM),
            pl.BlockSpec((rows, w), lambda b, t: (b * ns + t, _cb('swa_q', w))),
            pl.BlockSpec((seq, LANES), lambda b, t: (b, _cb('swa_k'))),
            pl.BlockSpec((seq, LANES), lambda b, t: (b, _cb('swa_v'))),
            pl.BlockSpec((rows, w), lambda b, t: (b * ns + t, _cb('swa_z', w))),
        ],
        out_specs=pl.BlockSpec((rows, w), lambda b, t: (b * ns + t, 0)),
        out_shape=jax.ShapeDtypeStruct((batch * seq, w), BF16),
        scratch_shapes=[plane, plane, plane],
        compiler_params=pltpu.CompilerParams(dimension_semantics=("parallel", "arbitrary")),
        name="swa",
    )(sink, slab, slab, slab, slab)


def _ret_kernel(qk_ref, v_ref, z_ref, o_ref, state_ref):
    c = RET_CHUNK

    @pl.when(pl.program_id(1) == 0)
    def _():
        state_ref[...] = jnp.zeros(state_ref.shape, F32)

    q = qk_ref[:, :RET_HEADS * RET_QK_DIM]
    k = qk_ref[:, RET_HEADS * RET_QK_DIM:]
    ii = lax.broadcasted_iota(jnp.int32, (c, c), 0)
    jj = lax.broadcasted_iota(jnp.int32, (c, c), 1)
    diff = (ii - jj).astype(F32)
    row = lax.broadcasted_iota(jnp.int32, (c, 1), 0).astype(F32)
    lane = lax.broadcasted_iota(jnp.int32, (c, RET_HEADS * RET_QK_DIM), 1)
    for h in range(RET_HEADS):
        log_g = math.log(1.0 - 2.0 ** (-5.0 - h))
        intra = jnp.where(diff >= 0, jnp.exp(jnp.maximum(diff, 0.0) * log_g), 0.0)
        q_dec = jnp.exp((row + 1.0) * log_g)
        k_dec = jnp.exp((c - 1.0 - row) * log_g)
        chunk_dec = math.exp(c * log_g)
        qm = jnp.where((lane // RET_QK_DIM) == h, q, jnp.zeros((), BF16))
        vh = v_ref[:, h * RET_V_DIM:(h + 1) * RET_V_DIM]
        att = _dot_nt(qm, k) * intra
        st = state_ref[h]
        o = _dot(att.astype(BF16), vh) + _dot(qm, st.astype(BF16)) * q_dec
        kd = (k.astype(F32) * k_dec).T.astype(BF16)
        state_ref[h] = st * chunk_dec + _dot(kd, vh)
        mu = jnp.mean(o, axis=-1, keepdims=True)
        var = jnp.mean(jnp.square(o - mu), axis=-1, keepdims=True)
        o = (o - mu) * lax.rsqrt(var + NORM_EPS)
        zh = z_ref[:, h * RET_V_DIM:(h + 1) * RET_V_DIM].astype(F32)
        o_ref[:, h * RET_V_DIM:(h + 1) * RET_V_DIM] = (o * _silu(zh)).astype(BF16)


def _retention(slab, batch, seq):
    nc = seq // RET_CHUNK
    w = RET_HEADS * RET_V_DIM
    return pl.pallas_call(
        _ret_kernel,
        grid=(batch, nc),
        in_specs=[
            pl.BlockSpec((RET_CHUNK, w), lambda b, c: (b * nc + c, _cb('ret_q', w))),
            pl.BlockSpec((RET_CHUNK, w), lambda b, c: (b * nc + c, _cb('ret_v', w))),
            pl.BlockSpec((RET_CHUNK, w), lambda b, c: (b * nc + c, _cb('ret_z', w))),
        ],
        out_specs=pl.BlockSpec((RET_CHUNK, w), lambda b, c: (b * nc + c, 0)),
        out_shape=jax.ShapeDtypeStruct((batch * seq, w), BF16),
        scratch_shapes=[pltpu.VMEM((RET_HEADS, RET_HEADS * RET_QK_DIM, RET_V_DIM), F32)],
        compiler_params=pltpu.CompilerParams(dimension_semantics=("parallel", "arbitrary")),
        name="retention",
    )(slab, slab, slab)


def _compress_kernel(r_ref, w1_ref, w2_ref, pos_ref, cos_ref, sa_ref, sb_ref, o_ref):
    kind = pl.program_id(0) // NSA_KV_HEADS
    half = NSA_CMP_STRIDE * HEAD_DIM
    r = r_ref[...]
    n_rows = r.shape[0]
    lo = _dot(r, w1_ref[:half, :])
    hi = _dot(r, w1_ref[half:, :])
    pos = _dot(pos_ref[...], w1_ref[...])[0:1, :]
    pre = lo + pltpu.roll(hi, n_rows - 1, 0) + pos
    y = _dot(jax.nn.gelu(pre).astype(BF16), w2_ref[...])
    roped = _rope_chunk(y, cos_ref[...], sa_ref[...], sb_ref[...])
    o_ref[...] = jnp.where(kind == 0, roped, y)


def _compress(r, w1, w2, pos8, cos_c, sa_c, sb_c, batch):
    n_rows = r.shape[2]
    flat = NSA_CMP_LEN * HEAD_DIM
    return pl.pallas_call(
        _compress_kernel,
        grid=(2 * NSA_KV_HEADS, batch),
        in_specs=[
            pl.BlockSpec((None, None, n_rows, flat // 2), lambda i, b: (i, b, 0, 0)),
            pl.BlockSpec((None, flat, NSA_CMP_HIDDEN), lambda i, b: (i // NSA_KV_HEADS, 0, 0)),
            pl.BlockSpec((None, NSA_CMP_HIDDEN, LANES), lambda i, b: (i // NSA_KV_HEADS, 0, 0)),
            pl.BlockSpec((8, flat), lambda i, b: (0, 0)),
            pl.BlockSpec((n_rows, LANES), lambda i, b: (0, 0)),
            pl.BlockSpec((n_rows, LANES), lambda i, b: (0, 0)),
            pl.BlockSpec((n_rows, LANES), lambda i, b: (0, 0)),
        ],
        out_specs=pl.BlockSpec((None, None, n_rows, LANES), lambda i, b: (i, b, 0, 0)),
        out_shape=jax.ShapeDtypeStruct((2 * NSA_KV_HEADS, batch, n_rows, LANES), F32),
        compiler_params=pltpu.CompilerParams(dimension_semantics=("parallel", "parallel")),
        name="nsa_compress",
    )(r, w1, w2, pos8, cos_c, sa_c, sb_c)


NSA_WIN_SPAN = TQ + NSA_WINDOW
SEL_LANE0 = 64


def _nsa_kernel(q_ref, ks_ref, vs_ref, kw_ref, vw_ref, kc_ref, vc_ref, g_ref, z_ref, o_ref,
                ksa_ref, ksb_ref, vsa_ref, vsb_ref, kwr_ref, vwa_ref, vwb_ref, kcr_ref, vcr_ref,
                ocmp_ref, bias_ref, *, seq):
    hk = pl.program_id(1)
    t = pl.program_id(2)
    n_slc = seq // NSA_SLC_BLOCK
    n_cmp = (seq - NSA_CMP_LEN) // NSA_CMP_STRIDE + 1
    n_crow = kc_ref.shape[0]
    w = GQA_W

    @pl.when(t == 0)
    def _():
        ks = _rep2(ks_ref[...].astype(F32), hk).astype(BF16)
        rowblk = lax.broadcasted_iota(jnp.int32, (seq, LANES), 0) // NSA_SLC_BLOCK
        lane = lax.broadcasted_iota(jnp.int32, (seq, LANES), 1)
        onehot = jnp.where(lane - SEL_LANE0 == rowblk, 1.0, 0.0).astype(BF16)
        ksa_ref[...] = jnp.concatenate([ks, onehot], axis=1)
        ksb_ref[...] = jnp.concatenate([onehot, ks], axis=1)
        vsa_ref[...], vsb_ref[...] = _value_variants(_rep2(vs_ref[...].astype(F32), hk))
        kwin = _rep2(kw_ref[...].astype(F32), hk).astype(BF16)
        kwr_ref[...] = jnp.concatenate([kwin, kwin], axis=1)
        vwa_ref[...], vwb_ref[...] = _value_variants(_rep2(vw_ref[...].astype(F32), hk))
        kc = kc_ref[...]
        kc = (kc + pltpu.roll(kc, HEAD_DIM, 1)).astype(BF16)
        kcr_ref[...] = jnp.concatenate([kc, kc], axis=1)
        vc = vc_ref[...]
        vc = (vc + pltpu.roll(vc, HEAD_DIM, 1)).astype(BF16)
        vcr_ref[...] = jnp.concatenate([vc, vc], axis=1)

        q_all = q_ref[...]
        pos_all = lax.broadcasted_iota(jnp.int32, (seq, 1), 0)
        ci = lax.broadcasted_iota(jnp.int32, (1, n_crow), 1)
        cmask = jnp.logical_and(ci * NSA_CMP_STRIDE + NSA_CMP_LEN - 1 <= pos_all, ci < n_cmp)
        kcr, vcr = kcr_ref[...], vcr_ref[...]
        p_sum = jnp.zeros((seq, n_crow), F32)
        o_cmp = jnp.zeros((seq, w), F32)
        for g in range(4):
            mask_g = _head_mask(g, w, seq)
            s = jnp.where(cmask, _dot_nt(jnp.where(mask_g, q_all, jnp.zeros((), BF16)), kcr), NEG)
            m = jnp.max(s, axis=-1, keepdims=True)
            e = jnp.where(cmask, jnp.exp2(s - m), 0.0)
            p = e * (1.0 / jnp.maximum(jnp.sum(e, axis=-1, keepdims=True), TINY))
            p_sum = p_sum + p
            o_cmp = jnp.where(mask_g, _dot(p.astype(BF16), vcr), o_cmp)
        ocmp_ref[...] = o_cmp

        c0 = lax.broadcasted_iota(jnp.int32, (n_slc, n_crow), 1) * NSA_CMP_STRIDE
        s0 = lax.broadcasted_iota(jnp.int32, (n_slc, n_crow), 0) * NSA_SLC_BLOCK
        ov = jnp.minimum(c0 + NSA_CMP_LEN, s0 + NSA_SLC_BLOCK) - jnp.maximum(c0, s0)
        ov = (jnp.maximum(ov, 0).astype(F32) * (1.0 / NSA_CMP_LEN)).astype(BF16)
        imp = _dot_nt(jnp.concatenate([ov, ov, ov], axis=1),
                      jnp.concatenate(_split3(p_sum), axis=1).astype(BF16))
        j_iota = lax.broadcasted_iota(jnp.int32, (n_slc, seq), 0)
        q_blk = lax.broadcasted_iota(jnp.int32, (n_slc, seq), 1) // NSA_SLC_BLOCK
        forced = jnp.logical_or(j_iota == 0, jnp.logical_or(j_iota == q_blk, j_iota == q_blk - 1))
        imp = jnp.where(forced, jnp.inf, imp)
        imp = jnp.where(j_iota <= q_blk, imp, -jnp.inf)
        cnt = _rank_before(imp, n_slc)
        allowed = jnp.logical_and(cnt < min(NSA_SLC_TOPN, n_slc), j_iota <= q_blk)
        bias_ref[...] = _bias_lanes(jnp.where(allowed, 1.0, 0.0), SEL_LANE0)

    rows = pl.ds(pl.multiple_of(t * TQ, TQ), TQ)
    q = q_ref[rows, :]
    bias = bias_ref[rows, :]
    zero = jnp.zeros((), BF16)
    masks = [_head_mask(g, w) for g in range(4)]
    qms = [jnp.where(masks[g], q, zero) for g in range(4)]
    qaugs = []
    for g in range(4):
        half_mask = _head_mask(g % 2, LANES)
        if g < 2:
            qaugs.append(jnp.concatenate([jnp.where(half_mask, q[:, :LANES], zero), bias], axis=1))
        else:
            qaugs.append(jnp.concatenate([bias, jnp.where(half_mask, q[:, LANES:], zero)], axis=1))

    gates = jax.nn.sigmoid(g_ref[...].astype(F32))
    g_hi = gates.astype(BF16)
    g_lo = (gates - g_hi.astype(F32)).astype(BF16)
    er = lax.broadcasted_iota(jnp.int32, (2 * LANES, 3 * w), 0) % LANES
    ec = lax.broadcasted_iota(jnp.int32, (2 * LANES, 3 * w), 1)
    expand = jnp.where(er == (ec // w) * NSA_HEADS + hk * 4 + (ec % w) // HEAD_DIM, 1.0, 0.0).astype(BF16)
    gx = _dot(jnp.concatenate([g_hi, g_lo], axis=1), expand)
    o_cmp = gx[:, :w] * ocmp_ref[rows, :]
    gate_z = _silu(z_ref[...].astype(F32))

    for tt in range(seq // TQ):
        @pl.when(t == tt)
        def _(tt=tt):
            k0 = max(tt * TQ - NSA_WINDOW, 0)
            kwin = kwr_ref[k0:k0 + NSA_WIN_SPAN, :]
            dist = (tt * TQ + lax.broadcasted_iota(jnp.int32, (TQ, 1), 0)) - (
                k0 + lax.broadcasted_iota(jnp.int32, (1, NSA_WIN_SPAN), 1))
            wmask = jnp.logical_and(dist >= 0, dist < NSA_WINDOW)
            o_win = jnp.zeros((TQ, w), F32)
            o_slc = jnp.zeros((TQ, w), F32)
            for g in range(4):
                vref, ones = (vwa_ref, ONES_A) if g < 3 else (vwb_ref, ONES_B)
                s = jnp.where(wmask, _dot_nt(qms[g], kwin), NEG)
                m = jnp.max(s, axis=-1, keepdims=True)
                o = _dot(jnp.exp2(s - m).astype(BF16), vref[k0:k0 + NSA_WIN_SPAN, :])
                o_win = jnp.where(masks[g], _normalize(o, ones), o_win)

                kref = ksa_ref if g < 2 else ksb_ref
                vref, ones = (vsa_ref, ONES_A) if g < 3 else (vsb_ref, ONES_B)
                o = _prefix_attention(
                    qaugs[g],
                    lambda start, size, kref=kref: kref[start:start + size, :],
                    lambda start, size, vref=vref: vref[start:start + size, :],
                    tt + 1, ones)
                o_slc = jnp.where(masks[g], o, o_slc)
            o = o_cmp + gx[:, w:2 * w] * o_slc + gx[:, 2 * w:] * o_win
            o_ref[...] = (o * gate_z).astype(BF16)


def _nsa(slab, cmp_kv, batch, seq):
    nt = seq // TQ
    w = GQA_W
    n_crow = cmp_kv.shape[2]
    kv_spec = lambda name: pl.BlockSpec((seq, LANES), lambda b, h, t: (b, _cb(name)))
    big = pltpu.VMEM((seq, w), BF16)
    return pl.pallas_call(
        functools.partial(_nsa_kernel, seq=seq),
        grid=(batch, NSA_KV_HEADS, nt),
        in_specs=[
            pl.BlockSpec((seq, w), lambda b, h, t: (b, _cb('nsa_q', w) + h)),
            kv_spec('nsa_ks'), kv_spec('nsa_vs'), kv_spec('nsa_kw'), kv_spec('nsa_vw'),
            pl.BlockSpec((None, None, n_crow, LANES), lambda b, h, t: (h, b, 0, 0)),
            pl.BlockSpec((None, None, n_crow, LANES), lambda b, h, t: (NSA_KV_HEADS + h, b, 0, 0)),
            pl.BlockSpec((TQ, LANES), lambda b, h, t: (b * nt + t, _cb('nsa_g'))),
            pl.BlockSpec((TQ, w), lambda b, h, t: (b * nt + t, _cb('nsa_z', w) + h)),
        ],
        out_specs=pl.BlockSpec((TQ, w), lambda b, h, t: (b * nt + t, h)),
        out_shape=jax.ShapeDtypeStruct((batch * seq, NSA_HEADS * HEAD_DIM), BF16),
        scratch_shapes=[
            big, big, big, big, big, big, big,
            pltpu.VMEM((n_crow, w), BF16), pltpu.VMEM((n_crow, w), BF16),
            pltpu.VMEM((seq, w), F32), pltpu.VMEM((seq, LANES), BF16),
        ],
        compiler_params=pltpu.CompilerParams(dimension_semantics=("parallel", "parallel", "arbitrary")),
        name="nsa",
    )(slab, slab, slab, slab, slab, cmp_kv, cmp_kv, slab, slab)


def _merge_kernel(x_ref, g_ref, oa_ref, ob_ref, oc_ref, od_ref, wb_ref, wo_ref, fg_ref, o_ref, *, final):
    merged = None
    row = 0
    for i, br in enumerate((oa_ref, ob_ref, oc_ref, od_ref)):
        width = br.shape[1]
        y = _dot(br[...], wb_ref[row:row + width, :])
        gate = jax.nn.sigmoid(g_ref[:, i * D_MODEL:(i + 1) * D_MODEL].astype(F32))
        merged = gate * y if merged is None else merged + gate * y
        row += width
    x = x_ref[...] + _dot(merged.astype(BF16), wo_ref[...])
    if final:
        ms = jnp.mean(x * x, axis=-1, keepdims=True)
        x = x * lax.rsqrt(ms + NORM_EPS) * fg_ref[...]
    o_ref[...] = x


def _merge(x2, slab, outs, w_branch, w_out, final_g, final):
    t = x2.shape[0]
    tm = MERGE_TM
    d_branch = w_branch.shape[0]
    row_spec = lambda width: pl.BlockSpec((tm, width), lambda i: (i, 0))
    return pl.pallas_call(
        functools.partial(_merge_kernel, final=final),
        grid=(t // tm,),
        in_specs=[
            row_spec(D_MODEL),
            pl.BlockSpec((tm, N_BRANCH * D_MODEL), lambda i: (i, _cb('merge_g', N_BRANCH * D_MODEL))),
            row_spec(outs[0].shape[1]), row_spec(outs[1].shape[1]),
            row_spec(outs[2].shape[1]), row_spec(outs[3].shape[1]),
            pl.BlockSpec((d_branch, D_MODEL), lambda i: (0, 0)),
            pl.BlockSpec((D_MODEL, D_MODEL), lambda i: (0, 0)),
            pl.BlockSpec((1, D_MODEL), lambda i: (0, 0)),
        ],
        out_specs=row_spec(D_MODEL),
        out_shape=jax.ShapeDtypeStruct((t, D_MODEL), F32),
        compiler_params=pltpu.CompilerParams(dimension_semantics=("parallel",)),
        name="merge_out",
    )(x2, slab, *outs, w_branch, w_out, final_g.reshape(1, D_MODEL))


def _rope_tables(positions):
    inv = 1.0 / (ROPE_THETA ** (jnp.arange(0, HEAD_DIM, 2, dtype=F32) / HEAD_DIM))
    ang = positions.astype(F32)[:, None] * inv[None, :]
    cos, sin = jnp.cos(ang), jnp.sin(ang)
    zero = jnp.zeros_like(sin)
    cos_t = jnp.tile(cos, (1, 4))
    sa_t = jnp.tile(jnp.concatenate([-sin, zero], axis=1), (1, 2))
    sb_t = jnp.tile(jnp.concatenate([zero, sin], axis=1), (1, 2))
    return cos_t, sa_t, sb_t


def kernel(x, norm_g, w_in, w_branch, w_out, swa_sink, nsa_cmp_pos, nsa_w_ck1, nsa_w_ck2,
           nsa_w_cv1, nsa_w_cv2, final_norm_g):
    batch, seq, d = x.shape
    depth = w_in.shape[0]
    assert d == D_MODEL and seq % TQ == 0 and seq % RET_CHUNK == 0 and (batch * seq) % MERGE_TM == 0
    assert seq >= NSA_WIN_SPAN and seq % (8 * NSA_CMP_STRIDE) == 0 and (seq // MOBA_BLOCK) % 8 == 0

    cos_t, sa_t, sb_t = _rope_tables(jnp.arange(seq))
    n_crow = seq // NSA_CMP_STRIDE
    cos_c, sa_c, sb_c = _rope_tables(jnp.arange(n_crow) * NSA_CMP_STRIDE + NSA_CMP_LEN - 1)

    x2 = x.reshape(batch * seq, d)
    for l in range(depth):
        slab, kcvc = _proj_in(x2, norm_g[l], _slab_weights(w_in[l]), cos_t, sa_t, sb_t, seq)

        groups = kcvc.reshape(2 * NSA_KV_HEADS, batch, n_crow, NSA_CMP_STRIDE * HEAD_DIM)
        w1 = jnp.stack([nsa_w_ck1[l], nsa_w_cv1[l]]).astype(BF16)
        w2 = jnp.pad(jnp.stack([nsa_w_ck2[l], nsa_w_cv2[l]]), ((0, 0), (0, 0), (0, LANES - HEAD_DIM))).astype(BF16)
        pos8 = jnp.broadcast_to(nsa_cmp_pos[l].reshape(1, -1), (8, NSA_CMP_LEN * HEAD_DIM)).astype(BF16)
        cmp_kv = _compress(groups, w1, w2, pos8, cos_c, sa_c, sb_c, batch)

        outs = (
            _moba(slab, batch, seq),
            _swa(slab, swa_sink[l], batch, seq),
            _retention(slab, batch, seq),
            _nsa(slab, cmp_kv, batch, seq),
        )
        x2 = _merge(x2, slab, outs, w_branch[l].astype(BF16), w_out[l].astype(BF16),
                    final_norm_g, final=(l == depth - 1))
    return x2.reshape(batch, seq, d)
```

```python
import functools
import math

import jax
import jax.numpy as jnp
from jax import lax
from jax.experimental import pallas as pl
from jax.experimental.pallas import tpu as pltpu

F32 = jnp.float32
BF16 = jnp.bfloat16

D_MODEL = 1024
HEAD_DIM = 64
ROPE_THETA = 10000.0
NORM_EPS = 1e-6
TINY = 1e-30
N_BRANCH = 4
NEG = -1e30
LOG2E = 1.4426950408889634

MOBA_HEADS = 8
MOBA_BLOCK = 256
MOBA_TOPK = 3

SWA_HEADS = 8
SWA_KV_HEADS = 2
SWA_WINDOW = 128

RET_HEADS = 4
RET_QK_DIM = 64
RET_V_DIM = 128

NSA_HEADS = 8
NSA_KV_HEADS = 2
NSA_CMP_LEN = 32
NSA_CMP_STRIDE = 16
NSA_CMP_HIDDEN = 256
NSA_SLC_BLOCK = 64
NSA_SLC_TOPN = 16
NSA_WINDOW = 512

LANES = 128
MXU_N = 256
TQ = 256
MOBA_PAIRS_PER_ITER = 2
PROJ_TM = 2048
PROJ_TN = 1024
MERGE_TM = 512
RET_CHUNK = 256

_IN_SPLITS = (
    ('moba_q', 512), ('moba_k', 512), ('moba_v', 512), ('moba_z', 512),
    ('swa_q', 512), ('swa_k', 128), ('swa_v', 128), ('swa_z', 512),
    ('ret_q', 256), ('ret_k', 256), ('ret_v', 512), ('ret_z', 512),
    ('nsa_q', 512), ('nsa_kc', 128), ('nsa_vc', 128),
    ('nsa_ks', 128), ('nsa_vs', 128), ('nsa_kw', 128), ('nsa_vw', 128),
    ('nsa_g', 24), ('nsa_z', 512),
    ('merge_g', 4096),
)
_WIDTH = dict(_IN_SPLITS)
_SLAB = (
    ('merge_g', 4096),
    ('moba_q', 512), ('moba_k', 512), ('swa_q', 512), ('nsa_q', 512),
    ('ret_q', 256), ('ret_k', 256),
    ('swa_k', 128), ('nsa_ks', 128), ('nsa_kw', 128), (None, 128),
    ('moba_v', 512), ('moba_z', 512), ('swa_z', 512), ('ret_v', 512), ('ret_z', 512), ('nsa_z', 512),
    ('swa_v', 128), ('nsa_vs', 128), ('nsa_vw', 128), ('nsa_g', 128),
    ('nsa_kc', 128), ('nsa_vc', 128), (None, 256),
)
_COL_SCALE = {'moba_q': 0.125 * LOG2E, 'swa_q': 0.125 * LOG2E, 'nsa_q': 0.125 * LOG2E, 'ret_k': 0.125}


def _slab_layout():
    src_off, off = {}, 0
    for name, w in _IN_SPLITS:
        src_off[name] = off
        off += w
    col, pos = {}, 0
    for name, w in _SLAB:
        if name is not None:
            col[name] = pos
        pos += w
    return src_off, col, pos


_SRC_OFF, COL, D_SLAB = _slab_layout()


def _slab_weights(w):
    parts = []
    for name, width in _SLAB:
        if name is None:
            parts.append(jnp.zeros((w.shape[0], width), BF16))
            continue
        blk = w[:, _SRC_OFF[name]:_SRC_OFF[name] + _WIDTH[name]]
        if name in _COL_SCALE:
            blk = blk * _COL_SCALE[name]
        parts.append(jnp.pad(blk.astype(BF16), ((0, 0), (0, width - _WIDTH[name]))))
    return jnp.concatenate(parts, axis=1)
ROPE_COL_LO, ROPE_COL_HI = COL['moba_q'], COL['moba_v']
assert ROPE_COL_LO % PROJ_TN == 0 and ROPE_COL_HI % PROJ_TN == 0 and D_SLAB % PROJ_TN == 0
KCVC_TILE, KCVC_OFF = divmod(COL['nsa_kc'], PROJ_TN)


def _cb(name, width=LANES):
    assert COL[name] % width == 0
    return COL[name] // width


NT = (((1,), (1,)), ((), ()))


def _dot(a, b):
    return jnp.dot(a, b, preferred_element_type=F32)


def _dot_nt(a, b):
    return lax.dot_general(a, b, NT, preferred_element_type=F32)


def _silu(z):
    return z * jax.nn.sigmoid(z)


def _rope_chunk(y, cos, sin_a, sin_b):
    return y * cos + pltpu.roll(y, 96, 1) * sin_a + pltpu.roll(y, 32, 1) * sin_b


def _split3(x):
    x1 = x.astype(BF16).astype(F32)
    x2 = (x - x1).astype(BF16).astype(F32)
    x3 = (x - x1 - x2).astype(BF16).astype(F32)
    return x1, x2, x3


def _proj_kernel(x_ref, g_ref, w_ref, cos_ref, sa_ref, sb_ref, o_ref, kcvc_ref, h_ref):
    j = pl.program_id(1)

    @pl.when(j == 0)
    def _():
        x = x_ref[...]
        ms = jnp.mean(x * x, axis=-1, keepdims=True)
        h_ref[...] = (x * lax.rsqrt(ms + NORM_EPS) * g_ref[...]).astype(BF16)

    is_rope = jnp.logical_and(j >= ROPE_COL_LO // PROJ_TN, j < ROPE_COL_HI // PROJ_TN)

    @pl.when(is_rope)
    def _():
        cos, sa, sb = cos_ref[...], sa_ref[...], sb_ref[...]
        for c in range(PROJ_TN // MXU_N):
            y = _dot(h_ref[...], w_ref[:, c * MXU_N:(c + 1) * MXU_N])
            for half in range(MXU_N // LANES):
                lo = c * MXU_N + half * LANES
                o_ref[:, lo:lo + LANES] = _rope_chunk(y[:, half * LANES:(half + 1) * LANES], cos, sa, sb).astype(BF16)

    @pl.when(jnp.logical_not(is_rope))
    def _():
        y = _dot(h_ref[...], w_ref[...])
        o_ref[...] = y.astype(BF16)

        @pl.when(j == KCVC_TILE)
        def _():
            for c in range(2 * NSA_KV_HEADS):
                lo = KCVC_OFF + c * HEAD_DIM
                kcvc_ref[c] = y[:, lo:lo + HEAD_DIM].astype(BF16)


def _proj_in(x2, norm_g, w_slab, cos_t, sa_t, sb_t, seq):
    t = x2.shape[0]
    tm = min(PROJ_TM, seq)
    per_seq = seq // tm
    return pl.pallas_call(
        _proj_kernel,
        grid=(t // tm, D_SLAB // PROJ_TN),
        in_specs=[
            pl.BlockSpec((tm, D_MODEL), lambda i, j: (i, 0)),
            pl.BlockSpec((1, D_MODEL), lambda i, j: (0, 0)),
            pl.BlockSpec((D_MODEL, PROJ_TN), lambda i, j: (0, j)),
            pl.BlockSpec((tm, LANES), lambda i, j: (i % per_seq, 0)),
            pl.BlockSpec((tm, LANES), lambda i, j: (i % per_seq, 0)),
            pl.BlockSpec((tm, LANES), lambda i, j: (i % per_seq, 0)),
        ],
        out_specs=[
            pl.BlockSpec((tm, PROJ_TN), lambda i, j: (i, j)),
            pl.BlockSpec((2 * NSA_KV_HEADS, tm, HEAD_DIM), lambda i, j: (0, i, 0)),
        ],
        out_shape=[
            jax.ShapeDtypeStruct((t, D_SLAB), BF16),
            jax.ShapeDtypeStruct((2 * NSA_KV_HEADS, t, HEAD_DIM), BF16),
        ],
        scratch_shapes=[pltpu.VMEM((tm, D_MODEL), BF16)],
        compiler_params=pltpu.CompilerParams(dimension_semantics=("parallel", "arbitrary")),
        name="proj_in",
    )(x2, norm_g.reshape(1, D_MODEL), w_slab, cos_t, sa_t, sb_t)


def _rank_before(scores, n):
    j_iota = lax.broadcasted_iota(jnp.int32, scores.shape, 0)
    cnt = jnp.zeros(scores.shape, F32)
    for jp in range(n):
        r = scores[jp:jp + 1, :]
        ahead = jnp.logical_or(r > scores, jnp.logical_and(r == scores, jp < j_iota))
        cnt = cnt + jnp.where(ahead, 1.0, 0.0)
    return cnt


def _normalize(o, ones_lane):
    return o * (1.0 / jnp.maximum(o[:, ones_lane:ones_lane + 1], TINY))


def _prefix_scores(qaugs, k_ats, n_tiles):
    n_past = (n_tiles - 1) * TQ
    row = lax.broadcasted_iota(jnp.int32, (TQ, TQ), 0)
    col = lax.broadcasted_iota(jnp.int32, (TQ, TQ), 1)
    s_ds = [jnp.where(col <= row, _dot_nt(q, k(n_past, TQ)), NEG) for q, k in zip(qaugs, k_ats)]
    s_ps = [_dot_nt(q, k(0, n_past)) if n_past else None for q, k in zip(qaugs, k_ats)]
    ms = [jnp.max(s, axis=-1, keepdims=True) for s in s_ds]
    if n_past:
        ms = [jnp.maximum(m, jnp.max(s, axis=-1, keepdims=True)) for m, s in zip(ms, s_ps)]
    return list(zip(s_ds, s_ps, ms))


def _prefix_values(scores, v_ats, n_tiles, ones_lanes):
    n_past = (n_tiles - 1) * TQ
    outs = []
    for (s_d, s_p, m), v_at, ones in zip(scores, v_ats, ones_lanes):
        o = _dot(jnp.exp2(s_d - m).astype(BF16), v_at(n_past, TQ))
        if n_past:
            o = o + _dot(jnp.exp2(s_p - m).astype(BF16), v_at(0, n_past))
        outs.append(_normalize(o, ones))
    return outs


def _prefix_attention_multi(qaugs, k_ats, v_ats, n_tiles, ones_lanes):
    return _prefix_values(_prefix_scores(qaugs, k_ats, n_tiles), v_ats, n_tiles, ones_lanes)


def _bias_lanes(allowed_t, row0):
    n, rows = allowed_t.shape
    pieces = []
    if row0:
        pieces.append(jnp.zeros((row0, rows), F32))
    pieces.append(allowed_t)
    if LANES - row0 - n:
        pieces.append(jnp.zeros((LANES - row0 - n, rows), F32))
    full = jnp.concatenate(pieces, axis=0)
    return ((full.T - 1.0) * (-NEG)).astype(BF16)


def _moba_kernel(q_ref, k_ref, v_ref, z_ref, o_ref, qp_ref, kaug_ref, vaug_ref, bias_ref, osc_ref, *, seq):
    t = pl.program_id(1)
    nb = seq // MOBA_BLOCK
    n_pair = MOBA_HEADS // 2

    @pl.when(t == 0)
    def _():
        for p in range(n_pair):
            sl = slice(p * LANES, (p + 1) * LANES)
            qp_ref[p] = q_ref[:, sl]
            kaug_ref[2 * p] = k_ref[:, sl]
            vaug_ref[2 * p] = v_ref[:, sl]
        rowblk = lax.broadcasted_iota(jnp.int32, (seq, LANES), 0) // MOBA_BLOCK
        lane = lax.broadcasted_iota(jnp.int32, (seq, LANES), 1)
        low = lane < HEAD_DIM
        blk = lax.broadcasted_iota(jnp.int32, (nb, seq), 0)
        own = lax.broadcasted_iota(jnp.int32, (nb, seq), 1) // MOBA_BLOCK
        avg = jnp.where(blk == own, 1.0 / MOBA_BLOCK, 0.0).astype(BF16)

        keep0 = jnp.where(low, 1.0, 0.0).astype(BF16)
        keep1 = jnp.where(low, 0.0, 1.0).astype(BF16)
        hot0 = jnp.where(lane - HEAD_DIM == rowblk, 1.0, 0.0).astype(BF16)
        hot1 = jnp.where(lane == rowblk, 1.0, 0.0).astype(BF16)

        def build(p, carry):
            k = kaug_ref[2 * p]
            v = vaug_ref[2 * p]
            kaug_ref[2 * p] = k * keep0 + hot0
            kaug_ref[2 * p + 1] = k * keep1 + hot1
            vaug_ref[2 * p] = v * keep0 + keep1
            vaug_ref[2 * p + 1] = v * keep1 + keep0
            km3 = jnp.concatenate(_split3(_dot(avg, k)), axis=0).astype(BF16)
            q_all = qp_ref[p]
            for h in range(2):
                mine = low if h == 0 else jnp.logical_not(low)
                g3 = _dot_nt(km3, jnp.where(mine, q_all, jnp.zeros((), BF16)))
                gate = g3[:nb] + g3[nb:2 * nb] + g3[2 * nb:]
                gate = jnp.where(blk < own, gate, -jnp.inf)
                cnt = _rank_before(gate, nb)
                allowed = jnp.logical_or(jnp.logical_and(cnt < MOBA_TOPK, blk < own), blk == own)
                bias_ref[2 * p + h] = _bias_lanes(jnp.where(allowed, 1.0, 0.0), HEAD_DIM if h == 0 else 0)
            return carry

        lax.fori_loop(0, n_pair, build, 0)

    rows = pl.ds(pl.multiple_of(t * TQ, TQ), TQ)
    lane = lax.broadcasted_iota(jnp.int32, (TQ, LANES), 1)

    for tt in range(nb):
        @pl.when(t == tt)
        def _(tt=tt):
            def body(it, carry):
                heads = [2 * (it * MOBA_PAIRS_PER_ITER + pp) + h for pp in range(MOBA_PAIRS_PER_ITER) for h in range(2)]
                qaugs = []
                for n, i in enumerate(heads):
                    q = qp_ref[i // 2, rows, :]
                    mine = (lane < HEAD_DIM) if n % 2 == 0 else (lane >= HEAD_DIM)
                    qaugs.append(jnp.where(mine, q, bias_ref[i, rows, :]))
                outs = _prefix_attention_multi(
                    qaugs,
                    [lambda start, size, i=i: kaug_ref[i, pl.ds(start, size), :] for i in heads],
                    [lambda start, size, i=i: vaug_ref[i, pl.ds(start, size), :] for i in heads],
                    tt + 1, [HEAD_DIM if n % 2 == 0 else 0 for n in range(len(heads))])
                for pp in range(MOBA_PAIRS_PER_ITER):
                    osc_ref[it * MOBA_PAIRS_PER_ITER + pp] = jnp.where(lane < HEAD_DIM, outs[2 * pp], outs[2 * pp + 1])
                return carry

            lax.fori_loop(0, n_pair // MOBA_PAIRS_PER_ITER, body, 0)

    gate_z = _silu(z_ref[...].astype(F32))
    for p in range(n_pair):
        sl = slice(p * LANES, (p + 1) * LANES)
        o_ref[:, sl] = (osc_ref[p] * gate_z[:, sl]).astype(BF16)


def _moba(slab, batch, seq):
    nt = seq // TQ
    w = MOBA_HEADS * HEAD_DIM
    n_pair = MOBA_HEADS // 2
    plane = lambda n: pltpu.VMEM((n, seq, LANES), BF16)
    return pl.pallas_call(
        functools.partial(_moba_kernel, seq=seq),
        grid=(batch, nt),
        in_specs=[
            pl.BlockSpec((seq, w), lambda b, t: (b, _cb('moba_q', w))),
            pl.BlockSpec((seq, w), lambda b, t: (b, _cb('moba_k', w))),
            pl.BlockSpec((seq, w), lambda b, t: (b, _cb('moba_v', w))),
            pl.BlockSpec((TQ, w), lambda b, t: (b * nt + t, _cb('moba_z', w))),
        ],
        out_specs=pl.BlockSpec((TQ, w), lambda b, t: (b * nt + t, 0)),
        out_shape=jax.ShapeDtypeStruct((batch * seq, w), BF16),
        scratch_shapes=[
            plane(n_pair), plane(MOBA_HEADS), plane(MOBA_HEADS), plane(MOBA_HEADS),
            pltpu.VMEM((n_pair, TQ, LANES), F32),
        ],
        compiler_params=pltpu.CompilerParams(dimension_semantics=("parallel", "arbitrary")),
        name="moba",
    )(slab, slab, slab, slab)


GQA_W = 4 * HEAD_DIM
ONES_A, ONES_B = 3 * HEAD_DIM, 0


def _rep2(x, hk):
    lane = lax.broadcasted_iota(jnp.int32, x.shape, 1)
    keep = (lane < HEAD_DIM) == (hk == 0)
    return jnp.where(keep, x, pltpu.roll(x, HEAD_DIM, 1))


def _value_variants(v2):
    low = lax.broadcasted_iota(jnp.int32, v2.shape, 1) < HEAD_DIM
    va = jnp.concatenate([v2, jnp.where(low, v2, 1.0)], axis=1).astype(BF16)
    vb = jnp.concatenate([jnp.where(low, 1.0, v2), v2], axis=1).astype(BF16)
    return va, vb


def _head_mask(g, width, rows=TQ):
    lane = lax.broadcasted_iota(jnp.int32, (rows, width), 1)
    return (lane // HEAD_DIM) == g


SWA_SPAN = TQ + LANES
SWA_TILES_PER_STEP = 2


def _swa_kernel(sink_ref, q_ref, k_ref, v_ref, z_ref, o_ref, krep_ref, va_ref, vb_ref):
    t = pl.program_id(1)

    @pl.when(t == 0)
    def _():
        for hk in range(SWA_KV_HEADS):
            kr = _rep2(k_ref[...].astype(F32), hk).astype(BF16)
            krep_ref[hk] = jnp.concatenate([kr, kr], axis=1)
            va_ref[hk], vb_ref[hk] = _value_variants(_rep2(v_ref[...].astype(F32), hk))

    gate_z = _silu(z_ref[...].astype(F32))
    masks = [_head_mask(g, GQA_W) for g in range(4)]
    chains = []
    for sub in range(SWA_TILES_PER_STEP):
        r0 = (t * SWA_TILES_PER_STEP + sub) * TQ
        k0 = pl.multiple_of(jnp.maximum(r0 - LANES, 0), LANES)
        dist = (r0 + lax.broadcasted_iota(jnp.int32, (TQ, SWA_SPAN), 0)) - (
            k0 + lax.broadcasted_iota(jnp.int32, (TQ, SWA_SPAN), 1))
        mask = jnp.logical_and(dist >= 0, dist < SWA_WINDOW)
        for hk in range(SWA_KV_HEADS):
            q = q_ref[sub * TQ:(sub + 1) * TQ, hk * GQA_W:(hk + 1) * GQA_W]
            kw = krep_ref[hk, pl.ds(k0, SWA_SPAN), :]
            for g in range(4):
                s = jnp.where(mask, _dot_nt(jnp.where(masks[g], q, jnp.zeros((), BF16)), kw), NEG)
                sink = sink_ref[hk * 4 + g] * LOG2E
                m = jnp.maximum(jnp.max(s, axis=-1, keepdims=True), sink)
                chains.append((sub, hk, g, s, m, sink, k0))
    outs = {}
    for sub, hk, g, s, m, sink, k0 in chains:
        vref, ones = (va_ref, ONES_A) if g < 3 else (vb_ref, ONES_B)
        o = _dot(jnp.exp2(s - m).astype(BF16), vref[hk, pl.ds(k0, SWA_SPAN), :])
        den = o[:, ones:ones + 1] + jnp.exp2(sink - m)
        o = o * (1.0 / jnp.maximum(den, TINY))
        outs[(sub, hk)] = o if g == 0 else jnp.where(masks[g], o, outs[(sub, hk)])
    for (sub, hk), out in outs.items():
        rows, cols = slice(sub * TQ, (sub + 1) * TQ), slice(hk * GQA_W, (hk + 1) * GQA_W)
        o_ref[rows, cols] = (out * gate_z[rows, cols]).astype(BF16)


def _swa(slab, sink, batch, seq):
    rows = TQ * SWA_TILES_PER_STEP
    ns = seq // rows
    w = SWA_HEADS * HEAD_DIM
    plane = pltpu.VMEM((SWA_KV_HEADS, seq, GQA_W), BF16)
    return pl.pallas_call(
        _swa_kernel,
        grid=(batch, ns),
        in_specs=[
            pl.BlockSpec(memory_space=pltpu.SMEM),
            pl.BlockSpec((rows, w), lambda b, t: (b * ns + t, _cb('swa_q', w))),
            pl.BlockSpec((seq, LANES), lambda b, t: (b, _cb('swa_k'))),
            pl.BlockSpec((seq, LANES), lambda b, t: (b, _cb('swa_v'))),
            pl.BlockSpec((rows, w), lambda b, t: (b * ns + t, _cb('swa_z', w))),
        ],
        out_specs=pl.BlockSpec((rows, w), lambda b, t: (b * ns + t, 0)),
        out_shape=jax.ShapeDtypeStruct((batch * seq, w), BF16),
        scratch_shapes=[plane, plane, plane],
        compiler_params=pltpu.CompilerParams(dimension_semantics=("parallel", "arbitrary")),
        name="swa",
    )(sink, slab, slab, slab, slab)


def _ret_kernel(qk_ref, v_ref, z_ref, o_ref, state_ref):
    c = RET_CHUNK

    @pl.when(pl.program_id(1) == 0)
    def _():
        state_ref[...] = jnp.zeros(state_ref.shape, F32)

    q = qk_ref[:, :RET_HEADS * RET_QK_DIM]
    k = qk_ref[:, RET_HEADS * RET_QK_DIM:]
    ii = lax.broadcasted_iota(jnp.int32, (c, c), 0)
    jj = lax.broadcasted_iota(jnp.int32, (c, c), 1)
    diff = (ii - jj).astype(F32)
    row = lax.broadcasted_iota(jnp.int32, (c, 1), 0).astype(F32)
    lane = lax.broadcasted_iota(jnp.int32, (c, RET_HEADS * RET_QK_DIM), 1)
    for h in range(RET_HEADS):
        log_g = math.log(1.0 - 2.0 ** (-5.0 - h))
        intra = jnp.where(diff >= 0, jnp.exp(jnp.maximum(diff, 0.0) * log_g), 0.0)
        q_dec = jnp.exp((row + 1.0) * log_g)
        k_dec = jnp.exp((c - 1.0 - row) * log_g)
        chunk_dec = math.exp(c * log_g)
        qm = jnp.where((lane // RET_QK_DIM) == h, q, jnp.zeros((), BF16))
        vh = v_ref[:, h * RET_V_DIM:(h + 1) * RET_V_DIM]
        att = _dot_nt(qm, k) * intra
        st = state_ref[h]
        o = _dot(att.astype(BF16), vh) + _dot(qm, st.astype(BF16)) * q_dec
        kd = (k.astype(F32) * k_dec).T.astype(BF16)
        state_ref[h] = st * chunk_dec + _dot(kd, vh)
        mu = jnp.mean(o, axis=-1, keepdims=True)
        var = jnp.mean(jnp.square(o - mu), axis=-1, keepdims=True)
        o = (o - mu) * lax.rsqrt(var + NORM_EPS)
        zh = z_ref[:, h * RET_V_DIM:(h + 1) * RET_V_DIM].astype(F32)
        o_ref[:, h * RET_V_DIM:(h + 1) * RET_V_DIM] = (o * _silu(zh)).astype(BF16)


def _retention(slab, batch, seq):
    nc = seq // RET_CHUNK
    w = RET_HEADS * RET_V_DIM
    return pl.pallas_call(
        _ret_kernel,
        grid=(batch, nc),
        in_specs=[
            pl.BlockSpec((RET_CHUNK, w), lambda b, c: (b * nc + c, _cb('ret_q', w))),
            pl.BlockSpec((RET_CHUNK, w), lambda b, c: (b * nc + c, _cb('ret_v', w))),
            pl.BlockSpec((RET_CHUNK, w), lambda b, c: (b * nc + c, _cb('ret_z', w))),
        ],
        out_specs=pl.BlockSpec((RET_CHUNK, w), lambda b, c: (b * nc + c, 0)),
        out_shape=jax.ShapeDtypeStruct((batch * seq, w), BF16),
        scratch_shapes=[pltpu.VMEM((RET_HEADS, RET_HEADS * RET_QK_DIM, RET_V_DIM), F32)],
        compiler_params=pltpu.CompilerParams(dimension_semantics=("parallel", "arbitrary")),
        name="retention",
    )(slab, slab, slab)


def _compress_kernel(r_ref, w1_ref, w2_ref, pos_ref, cos_ref, sa_ref, sb_ref, o_ref):
    kind = pl.program_id(0) // NSA_KV_HEADS
    half = NSA_CMP_STRIDE * HEAD_DIM
    r = r_ref[...]
    n_rows = r.shape[0]
    lo = _dot(r, w1_ref[:half, :])
    hi = _dot(r, w1_ref[half:, :])
    pos = _dot(pos_ref[...], w1_ref[...])[0:1, :]
    pre = lo + pltpu.roll(hi, n_rows - 1, 0) + pos
    y = _dot(jax.nn.gelu(pre).astype(BF16), w2_ref[...])
    roped = _rope_chunk(y, cos_ref[...], sa_ref[...], sb_ref[...])
    o_ref[...] = jnp.where(kind == 0, roped, y)


def _compress(r, w1, w2, pos8, cos_c, sa_c, sb_c, batch):
    n_rows = r.shape[2]
    flat = NSA_CMP_LEN * HEAD_DIM
    return pl.pallas_call(
        _compress_kernel,
        grid=(2 * NSA_KV_HEADS, batch),
        in_specs=[
            pl.BlockSpec((None, None, n_rows, flat // 2), lambda i, b: (i, b, 0, 0)),
            pl.BlockSpec((None, flat, NSA_CMP_HIDDEN), lambda i, b: (i // NSA_KV_HEADS, 0, 0)),
            pl.BlockSpec((None, NSA_CMP_HIDDEN, LANES), lambda i, b: (i // NSA_KV_HEADS, 0, 0)),
            pl.BlockSpec((8, flat), lambda i, b: (0, 0)),
            pl.BlockSpec((n_rows, LANES), lambda i, b: (0, 0)),
            pl.BlockSpec((n_rows, LANES), lambda i, b: (0, 0)),
            pl.BlockSpec((n_rows, LANES), lambda i, b: (0, 0)),
        ],
        out_specs=pl.BlockSpec((None, None, n_rows, LANES), lambda i, b: (i, b, 0, 0)),
        out_shape=jax.ShapeDtypeStruct((2 * NSA_KV_HEADS, batch, n_rows, LANES), F32),
        compiler_params=pltpu.CompilerParams(dimension_semantics=("parallel", "parallel")),
        name="nsa_compress",
    )(r, w1, w2, pos8, cos_c, sa_c, sb_c)


NSA_WIN_SPAN = TQ + NSA_WINDOW
SEL_LANE0 = 64


def _nsa_kernel(q_ref, ks_ref, vs_ref, kw_ref, vw_ref, kc_ref, vc_ref, g_ref, z_ref, o_ref,
                ksa_ref, ksb_ref, vsa_ref, vsb_ref, kwr_ref, vwa_ref, vwb_ref, kcr_ref, vcr_ref,
                ocmp_ref, bias_ref, *, seq):
    hk = pl.program_id(1)
    t = pl.program_id(2)
    n_slc = seq // NSA_SLC_BLOCK
    n_cmp = (seq - NSA_CMP_LEN) // NSA_CMP_STRIDE + 1
    n_crow = kc_ref.shape[0]
    w = GQA_W

    @pl.when(t == 0)
    def _():
        ks = _rep2(ks_ref[...].astype(F32), hk).astype(BF16)
        rowblk = lax.broadcasted_iota(jnp.int32, (seq, LANES), 0) // NSA_SLC_BLOCK
        lane = lax.broadcasted_iota(jnp.int32, (seq, LANES), 1)
        onehot = jnp.where(lane - SEL_LANE0 == rowblk, 1.0, 0.0).astype(BF16)
        ksa_ref[...] = jnp.concatenate([ks, onehot], axis=1)
        ksb_ref[...] = jnp.concatenate([onehot, ks], axis=1)
        vsa_ref[...], vsb_ref[...] = _value_variants(_rep2(vs_ref[...].astype(F32), hk))
        kwin = _rep2(kw_ref[...].astype(F32), hk).astype(BF16)
        kwr_ref[...] = jnp.concatenate([kwin, kwin], axis=1)
        vwa_ref[...], vwb_ref[...] = _value_variants(_rep2(vw_ref[...].astype(F32), hk))
        kc = kc_ref[...]
        kc = (kc + pltpu.roll(kc, HEAD_DIM, 1)).astype(BF16)
        kcr_ref[...] = jnp.concatenate([kc, kc], axis=1)
        vc = vc_ref[...]
        vc = (vc + pltpu.roll(vc, HEAD_DIM, 1)).astype(BF16)
        vcr_ref[...] = jnp.concatenate([vc, vc], axis=1)

        q_all = q_ref[...]
        pos_all = lax.broadcasted_iota(jnp.int32, (seq, 1), 0)
        ci = lax.broadcasted_iota(jnp.int32, (1, n_crow), 1)
        cmask = jnp.logical_and(ci * NSA_CMP_STRIDE + NSA_CMP_LEN - 1 <= pos_all, ci < n_cmp)
        kcr, vcr = kcr_ref[...], vcr_ref[...]
        cmasks = [_head_mask(g, w, seq) for g in range(4)]
        s_cs = [jnp.where(cmask, _dot_nt(jnp.where(cmasks[g], q_all, jnp.zeros((), BF16)), kcr), NEG) for g in range(4)]
        p_cs = []
        for s in s_cs:
            m = jnp.max(s, axis=-1, keepdims=True)
            e = jnp.where(cmask, jnp.exp2(s - m), 0.0)
            p_cs.append(e * (1.0 / jnp.maximum(jnp.sum(e, axis=-1, keepdims=True), TINY)))
        p_sum = (p_cs[0] + p_cs[1]) + (p_cs[2] + p_cs[3])
        o_cs = [_dot(p.astype(BF16), vcr) for p in p_cs]
        o_cmp = o_cs[0]
        for g in range(1, 4):
            o_cmp = jnp.where(cmasks[g], o_cs[g], o_cmp)
        ocmp_ref[...] = o_cmp

        c0 = lax.broadcasted_iota(jnp.int32, (n_slc, n_crow), 1) * NSA_CMP_STRIDE
        s0 = lax.broadcasted_iota(jnp.int32, (n_slc, n_crow), 0) * NSA_SLC_BLOCK
        ov = jnp.minimum(c0 + NSA_CMP_LEN, s0 + NSA_SLC_BLOCK) - jnp.maximum(c0, s0)
        ov = (jnp.maximum(ov, 0).astype(F32) * (1.0 / NSA_CMP_LEN)).astype(BF16)
        imp = _dot_nt(jnp.concatenate([ov, ov, ov], axis=1),
                      jnp.concatenate(_split3(p_sum), axis=1).astype(BF16))
        j_iota = lax.broadcasted_iota(jnp.int32, (n_slc, seq), 0)
        q_blk = lax.broadcasted_iota(jnp.int32, (n_slc, seq), 1) // NSA_SLC_BLOCK
        forced = jnp.logical_or(j_iota == 0, jnp.logical_or(j_iota == q_blk, j_iota == q_blk - 1))
        imp = jnp.where(forced, jnp.inf, imp)
        imp = jnp.where(j_iota <= q_blk, imp, -jnp.inf)
        cnt = _rank_before(imp, n_slc)
        allowed = jnp.logical_and(cnt < min(NSA_SLC_TOPN, n_slc), j_iota <= q_blk)
        bias_ref[...] = _bias_lanes(jnp.where(allowed, 1.0, 0.0), SEL_LANE0)

    rows = pl.ds(pl.multiple_of(t * TQ, TQ), TQ)
    q = q_ref[rows, :]
    bias = bias_ref[rows, :]
    zero = jnp.zeros((), BF16)
    masks = [_head_mask(g, w) for g in range(4)]
    qms = [jnp.where(masks[g], q, zero) for g in range(4)]
    qaugs = []
    for g in range(4):
        half_mask = _head_mask(g % 2, LANES)
        if g < 2:
            qaugs.append(jnp.concatenate([jnp.where(half_mask, q[:, :LANES], zero), bias], axis=1))
        else:
            qaugs.append(jnp.concatenate([bias, jnp.where(half_mask, q[:, LANES:], zero)], axis=1))

    gates = jax.nn.sigmoid(g_ref[...].astype(F32))
    g_hi = gates.astype(BF16)
    g_lo = (gates - g_hi.astype(F32)).astype(BF16)
    er = lax.broadcasted_iota(jnp.int32, (2 * LANES, 3 * w), 0) % LANES
    ec = lax.broadcasted_iota(jnp.int32, (2 * LANES, 3 * w), 1)
    expand = jnp.where(er == (ec // w) * NSA_HEADS + hk * 4 + (ec % w) // HEAD_DIM, 1.0, 0.0).astype(BF16)
    gx = _dot(jnp.concatenate([g_hi, g_lo], axis=1), expand)
    o_cmp = gx[:, :w] * ocmp_ref[rows, :]
    gate_z = _silu(z_ref[...].astype(F32))

    for tt in range(seq // TQ):
        @pl.when(t == tt)
        def _(tt=tt):
            k0 = max(tt * TQ - NSA_WINDOW, 0)
            kwin = kwr_ref[k0:k0 + NSA_WIN_SPAN, :]
            dist = (tt * TQ + lax.broadcasted_iota(jnp.int32, (TQ, 1), 0)) - (
                k0 + lax.broadcasted_iota(jnp.int32, (1, NSA_WIN_SPAN), 1))
            wmask = jnp.logical_and(dist >= 0, dist < NSA_WINDOW)
            s_ws = [jnp.where(wmask, _dot_nt(qms[g], kwin), NEG) for g in range(4)]
            k_s = [ksa_ref if g < 2 else ksb_ref for g in range(4)]
            sel_scores = _prefix_scores(qaugs, [lambda start, size, r=r: r[start:start + size, :] for r in k_s], tt + 1)
            m_ws = [jnp.max(s, axis=-1, keepdims=True) for s in s_ws]
            v_w = [(vwa_ref, ONES_A) if g < 3 else (vwb_ref, ONES_B) for g in range(4)]
            wins = [_normalize(_dot(jnp.exp2(s_ws[g] - m_ws[g]).astype(BF16), v_w[g][0][k0:k0 + NSA_WIN_SPAN, :]),
                               v_w[g][1]) for g in range(4)]
            v_s = [(vsa_ref, ONES_A) if g < 3 else (vsb_ref, ONES_B) for g in range(4)]
            slcs = _prefix_values(sel_scores, [lambda start, size, r=r[0]: r[start:start + size, :] for r in v_s],
                                  tt + 1, [r[1] for r in v_s])
            o_win, o_slc = wins[0], slcs[0]
            for g in range(1, 4):
                o_win = jnp.where(masks[g], wins[g], o_win)
                o_slc = jnp.where(masks[g], slcs[g], o_slc)
            o = o_cmp + gx[:, w:2 * w] * o_slc + gx[:, 2 * w:] * o_win
            o_ref[...] = (o * gate_z).astype(BF16)


def _nsa(slab, cmp_kv, batch, seq):
    nt = seq // TQ
    w = GQA_W
    n_crow = cmp_kv.shape[2]
    kv_spec = lambda name: pl.BlockSpec((seq, LANES), lambda b, h, t: (b, _cb(name)))
    big = pltpu.VMEM((seq, w), BF16)
    return pl.pallas_call(
        functools.partial(_nsa_kernel, seq=seq),
        grid=(batch, NSA_KV_HEADS, nt),
        in_specs=[
            pl.BlockSpec((seq, w), lambda b, h, t: (b, _cb('nsa_q', w) + h)),
            kv_spec('nsa_ks'), kv_spec('nsa_vs'), kv_spec('nsa_kw'), kv_spec('nsa_vw'),
            pl.BlockSpec((None, None, n_crow, LANES), lambda b, h, t: (h, b, 0, 0)),
            pl.BlockSpec((None, None, n_crow, LANES), lambda b, h, t: (NSA_KV_HEADS + h, b, 0, 0)),
            pl.BlockSpec((TQ, LANES), lambda b, h, t: (b * nt + t, _cb('nsa_g'))),
            pl.BlockSpec((TQ, w), lambda b, h, t: (b * nt + t, _cb('nsa_z', w) + h)),
        ],
        out_specs=pl.BlockSpec((TQ, w), lambda b, h, t: (b * nt + t, h)),
        out_shape=jax.ShapeDtypeStruct((batch * seq, NSA_HEADS * HEAD_DIM), BF16),
        scratch_shapes=[
            big, big, big, big, big, big, big,
            pltpu.VMEM((n_crow, w), BF16), pltpu.VMEM((n_crow, w), BF16),
            pltpu.VMEM((seq, w), F32), pltpu.VMEM((seq, LANES), BF16),
        ],
        compiler_params=pltpu.CompilerParams(dimension_semantics=("parallel", "parallel", "arbitrary")),
        name="nsa",
    )(slab, slab, slab, slab, slab, cmp_kv, cmp_kv, slab, slab)


def _merge_kernel(x_ref, g_ref, oa_ref, ob_ref, oc_ref, od_ref, wb_ref, wo_ref, fg_ref, o_ref, *, final):
    merged = None
    row = 0
    for i, br in enumerate((oa_ref, ob_ref, oc_ref, od_ref)):
        width = br.shape[1]
        y = _dot(br[...], wb_ref[row:row + width, :])
        gate = jax.nn.sigmoid(g_ref[:, i * D_MODEL:(i + 1) * D_MODEL].astype(F32))
        merged = gate * y if merged is None else merged + gate * y
        row += width
    x = x_ref[...] + _dot(merged.astype(BF16), wo_ref[...])
    if final:
        ms = jnp.mean(x * x, axis=-1, keepdims=True)
        x = x * lax.rsqrt(ms + NORM_EPS) * fg_ref[...]
    o_ref[...] = x


def _merge(x2, slab, outs, w_branch, w_out, final_g, final):
    t = x2.shape[0]
    tm = MERGE_TM
    d_branch = w_branch.shape[0]
    row_spec = lambda width: pl.BlockSpec((tm, width), lambda i: (i, 0))
    return pl.pallas_call(
        functools.partial(_merge_kernel, final=final),
        grid=(t // tm,),
        in_specs=[
            row_spec(D_MODEL),
            pl.BlockSpec((tm, N_BRANCH * D_MODEL), lambda i: (i, _cb('merge_g', N_BRANCH * D_MODEL))),
            row_spec(outs[0].shape[1]), row_spec(outs[1].shape[1]),
            row_spec(outs[2].shape[1]), row_spec(outs[3].shape[1]),
            pl.BlockSpec((d_branch, D_MODEL), lambda i: (0, 0)),
            pl.BlockSpec((D_MODEL, D_MODEL), lambda i: (0, 0)),
            pl.BlockSpec((1, D_MODEL), lambda i: (0, 0)),
        ],
        out_specs=row_spec(D_MODEL),
        out_shape=jax.ShapeDtypeStruct((t, D_MODEL), F32),
        compiler_params=pltpu.CompilerParams(dimension_semantics=("parallel",)),
        name="merge_out",
    )(x2, slab, *outs, w_branch, w_out, final_g.reshape(1, D_MODEL))


def _rope_tables(positions):
    inv = 1.0 / (ROPE_THETA ** (jnp.arange(0, HEAD_DIM, 2, dtype=F32) / HEAD_DIM))
    ang = positions.astype(F32)[:, None] * inv[None, :]
    cos, sin = jnp.cos(ang), jnp.sin(ang)
    zero = jnp.zeros_like(sin)
    cos_t = jnp.tile(cos, (1, 4))
    sa_t = jnp.tile(jnp.concatenate([-sin, zero], axis=1), (1, 2))
    sb_t = jnp.tile(jnp.concatenate([zero, sin], axis=1), (1, 2))
    return cos_t, sa_t, sb_t


def kernel(x, norm_g, w_in, w_branch, w_out, swa_sink, nsa_cmp_pos, nsa_w_ck1, nsa_w_ck2,
           nsa_w_cv1, nsa_w_cv2, final_norm_g):
    batch, seq, d = x.shape
    depth = w_in.shape[0]
    assert d == D_MODEL and seq % TQ == 0 and seq % RET_CHUNK == 0 and (batch * seq) % MERGE_TM == 0
    assert seq >= NSA_WIN_SPAN and seq % (8 * NSA_CMP_STRIDE) == 0 and (seq // MOBA_BLOCK) % 8 == 0

    cos_t, sa_t, sb_t = _rope_tables(jnp.arange(seq))
    n_crow = seq // NSA_CMP_STRIDE
    cos_c, sa_c, sb_c = _rope_tables(jnp.arange(n_crow) * NSA_CMP_STRIDE + NSA_CMP_LEN - 1)

    x2 = x.reshape(batch * seq, d)
    for l in range(depth):
        slab, kcvc = _proj_in(x2, norm_g[l], _slab_weights(w_in[l]), cos_t, sa_t, sb_t, seq)

        groups = kcvc.reshape(2 * NSA_KV_HEADS, batch, n_crow, NSA_CMP_STRIDE * HEAD_DIM)
        w1 = jnp.stack([nsa_w_ck1[l], nsa_w_cv1[l]]).astype(BF16)
        w2 = jnp.pad(jnp.stack([nsa_w_ck2[l], nsa_w_cv2[l]]), ((0, 0), (0, 0), (0, LANES - HEAD_DIM))).astype(BF16)
        pos8 = jnp.broadcast_to(nsa_cmp_pos[l].reshape(1, -1), (8, NSA_CMP_LEN * HEAD_DIM)).astype(BF16)
        cmp_kv = _compress(groups, w1, w2, pos8, cos_c, sa_c, sb_c, batch)

        outs = (
            _moba(slab, batch, seq),
            _swa(slab, swa_sink[l], batch, seq),
            _retention(slab, batch, seq),
            _nsa(slab, cmp_kv, batch, seq),
        )
        x2 = _merge(x2, slab, outs, w_branch[l].astype(BF16), w_out[l].astype(BF16),
                    final_norm_g, final=(l == depth - 1))
    return x2.reshape(batch, seq, d)
```

```python
import functools
import math

import jax
import jax.numpy as jnp
from jax import lax
from jax.experimental import pallas as pl
from jax.experimental.pallas import tpu as pltpu

F32 = jnp.float32
BF16 = jnp.bfloat16

D_MODEL = 1024
HEAD_DIM = 64
ROPE_THETA = 10000.0
NORM_EPS = 1e-6
TINY = 1e-30
N_BRANCH = 4
NEG = -1e30
LOG2E = 1.4426950408889634

MOBA_HEADS = 8
MOBA_BLOCK = 256
MOBA_TOPK = 3

SWA_HEADS = 8
SWA_KV_HEADS = 2
SWA_WINDOW = 128

RET_HEADS = 4
RET_QK_DIM = 64
RET_V_DIM = 128

NSA_HEADS = 8
NSA_KV_HEADS = 2
NSA_CMP_LEN = 32
NSA_CMP_STRIDE = 16
NSA_CMP_HIDDEN = 256
NSA_SLC_BLOCK = 64
NSA_SLC_TOPN = 16
NSA_WINDOW = 512

LANES = 128
MXU_N = 256
TQ = 256
MOBA_PAIRS_PER_ITER = 2
MOBA_TILES_PER_STEP = 2
PROJ_TM = 2048
PROJ_TN = 1024
MERGE_TM = 512
RET_CHUNK = 256

_IN_SPLITS = (
    ('moba_q', 512), ('moba_k', 512), ('moba_v', 512), ('moba_z', 512),
    ('swa_q', 512), ('swa_k', 128), ('swa_v', 128), ('swa_z', 512),
    ('ret_q', 256), ('ret_k', 256), ('ret_v', 512), ('ret_z', 512),
    ('nsa_q', 512), ('nsa_kc', 128), ('nsa_vc', 128),
    ('nsa_ks', 128), ('nsa_vs', 128), ('nsa_kw', 128), ('nsa_vw', 128),
    ('nsa_g', 24), ('nsa_z', 512),
    ('merge_g', 4096),
)
_WIDTH = dict(_IN_SPLITS)
_SLAB = (
    ('merge_g', 4096),
    ('moba_q', 512), ('moba_k', 512), ('swa_q', 512), ('nsa_q', 512),
    ('ret_q', 256), ('ret_k', 256),
    ('swa_k', 128), ('nsa_ks', 128), ('nsa_kw', 128), (None, 128),
    ('moba_v', 512), ('moba_z', 512), ('swa_z', 512), ('ret_v', 512), ('ret_z', 512), ('nsa_z', 512),
    ('swa_v', 128), ('nsa_vs', 128), ('nsa_vw', 128), ('nsa_g', 128),
    ('nsa_kc', 128), ('nsa_vc', 128), (None, 256),
)
_COL_SCALE = {'moba_q': 0.125 * LOG2E, 'swa_q': 0.125 * LOG2E, 'nsa_q': 0.125 * LOG2E, 'ret_k': 0.125}


def _slab_layout():
    src_off, off = {}, 0
    for name, w in _IN_SPLITS:
        src_off[name] = off
        off += w
    col, pos = {}, 0
    for name, w in _SLAB:
        if name is not None:
            col[name] = pos
        pos += w
    return src_off, col, pos


_SRC_OFF, COL, D_SLAB = _slab_layout()


def _slab_weights(w):
    parts = []
    for name, width in _SLAB:
        if name is None:
            parts.append(jnp.zeros((w.shape[0], width), BF16))
            continue
        blk = w[:, _SRC_OFF[name]:_SRC_OFF[name] + _WIDTH[name]]
        if name in _COL_SCALE:
            blk = blk * _COL_SCALE[name]
        parts.append(jnp.pad(blk.astype(BF16), ((0, 0), (0, width - _WIDTH[name]))))
    return jnp.concatenate(parts, axis=1)
ROPE_COL_LO, ROPE_COL_HI = COL['moba_q'], COL['moba_v']
assert ROPE_COL_LO % PROJ_TN == 0 and ROPE_COL_HI % PROJ_TN == 0 and D_SLAB % PROJ_TN == 0
KCVC_TILE, KCVC_OFF = divmod(COL['nsa_kc'], PROJ_TN)


def _cb(name, width=LANES):
    assert COL[name] % width == 0
    return COL[name] // width


NT = (((1,), (1,)), ((), ()))


def _dot(a, b):
    return jnp.dot(a, b, preferred_element_type=F32)


def _dot_nt(a, b):
    return lax.dot_general(a, b, NT, preferred_element_type=F32)


def _silu(z):
    return z * jax.nn.sigmoid(z)


def _rope_chunk(y, cos, sin_a, sin_b):
    return y * cos + pltpu.roll(y, 96, 1) * sin_a + pltpu.roll(y, 32, 1) * sin_b


def _split3(x):
    x1 = x.astype(BF16).astype(F32)
    x2 = (x - x1).astype(BF16).astype(F32)
    x3 = (x - x1 - x2).astype(BF16).astype(F32)
    return x1, x2, x3


def _proj_kernel(x_ref, g_ref, w_ref, cos_ref, sa_ref, sb_ref, o_ref, kcvc_ref, h_ref):
    j = pl.program_id(1)

    @pl.when(j == 0)
    def _():
        x = x_ref[...]
        ms = jnp.mean(x * x, axis=-1, keepdims=True)
        h_ref[...] = (x * lax.rsqrt(ms + NORM_EPS) * g_ref[...]).astype(BF16)

    is_rope = jnp.logical_and(j >= ROPE_COL_LO // PROJ_TN, j < ROPE_COL_HI // PROJ_TN)

    @pl.when(is_rope)
    def _():
        cos, sa, sb = cos_ref[...], sa_ref[...], sb_ref[...]
        for c in range(PROJ_TN // MXU_N):
            y = _dot(h_ref[...], w_ref[:, c * MXU_N:(c + 1) * MXU_N])
            for half in range(MXU_N // LANES):
                lo = c * MXU_N + half * LANES
                o_ref[:, lo:lo + LANES] = _rope_chunk(y[:, half * LANES:(half + 1) * LANES], cos, sa, sb).astype(BF16)

    @pl.when(jnp.logical_not(is_rope))
    def _():
        y = _dot(h_ref[...], w_ref[...])
        o_ref[...] = y.astype(BF16)

        @pl.when(j == KCVC_TILE)
        def _():
            for c in range(2 * NSA_KV_HEADS):
                lo = KCVC_OFF + c * HEAD_DIM
                kcvc_ref[c] = y[:, lo:lo + HEAD_DIM].astype(BF16)


def _proj_in(x2, norm_g, w_slab, cos_t, sa_t, sb_t, seq):
    t = x2.shape[0]
    tm = min(PROJ_TM, seq)
    per_seq = seq // tm
    return pl.pallas_call(
        _proj_kernel,
        grid=(t // tm, D_SLAB // PROJ_TN),
        in_specs=[
            pl.BlockSpec((tm, D_MODEL), lambda i, j: (i, 0)),
            pl.BlockSpec((1, D_MODEL), lambda i, j: (0, 0)),
            pl.BlockSpec((D_MODEL, PROJ_TN), lambda i, j: (0, j)),
            pl.BlockSpec((tm, LANES), lambda i, j: (i % per_seq, 0)),
            pl.BlockSpec((tm, LANES), lambda i, j: (i % per_seq, 0)),
            pl.BlockSpec((tm, LANES), lambda i, j: (i % per_seq, 0)),
        ],
        out_specs=[
            pl.BlockSpec((tm, PROJ_TN), lambda i, j: (i, j)),
            pl.BlockSpec((2 * NSA_KV_HEADS, tm, HEAD_DIM), lambda i, j: (0, i, 0)),
        ],
        out_shape=[
            jax.ShapeDtypeStruct((t, D_SLAB), BF16),
            jax.ShapeDtypeStruct((2 * NSA_KV_HEADS, t, HEAD_DIM), BF16),
        ],
        scratch_shapes=[pltpu.VMEM((tm, D_MODEL), BF16)],
        compiler_params=pltpu.CompilerParams(dimension_semantics=("parallel", "arbitrary")),
        name="proj_in",
    )(x2, norm_g.reshape(1, D_MODEL), w_slab, cos_t, sa_t, sb_t)


def _rank_before(scores, n):
    j_iota = lax.broadcasted_iota(jnp.int32, scores.shape, 0)
    cnt = jnp.zeros(scores.shape, F32)
    for jp in range(n):
        r = scores[jp:jp + 1, :]
        ahead = jnp.logical_or(r > scores, jnp.logical_and(r == scores, jp < j_iota))
        cnt = cnt + jnp.where(ahead, 1.0, 0.0)
    return cnt


def _normalize(o, ones_lane):
    return o * (1.0 / jnp.maximum(o[:, ones_lane:ones_lane + 1], TINY))


def _prefix_scores(qaugs, k_ats, n_tiles):
    n_past = (n_tiles - 1) * TQ
    row = lax.broadcasted_iota(jnp.int32, (TQ, TQ), 0)
    col = lax.broadcasted_iota(jnp.int32, (TQ, TQ), 1)
    s_ds = [jnp.where(col <= row, _dot_nt(q, k(n_past, TQ)), NEG) for q, k in zip(qaugs, k_ats)]
    s_ps = [_dot_nt(q, k(0, n_past)) if n_past else None for q, k in zip(qaugs, k_ats)]
    ms = [jnp.max(s, axis=-1, keepdims=True) for s in s_ds]
    if n_past:
        ms = [jnp.maximum(m, jnp.max(s, axis=-1, keepdims=True)) for m, s in zip(ms, s_ps)]
    return list(zip(s_ds, s_ps, ms))


def _prefix_values(scores, v_ats, n_tiles, ones_lanes):
    n_past = (n_tiles - 1) * TQ
    outs = []
    for (s_d, s_p, m), v_at, ones in zip(scores, v_ats, ones_lanes):
        o = _dot(jnp.exp2(s_d - m).astype(BF16), v_at(n_past, TQ))
        if n_past:
            o = o + _dot(jnp.exp2(s_p - m).astype(BF16), v_at(0, n_past))
        outs.append(_normalize(o, ones))
    return outs


def _bias_lanes(allowed_t, row0):
    n, rows = allowed_t.shape
    pieces = []
    if row0:
        pieces.append(jnp.zeros((row0, rows), F32))
    pieces.append(allowed_t)
    if LANES - row0 - n:
        pieces.append(jnp.zeros((LANES - row0 - n, rows), F32))
    full = jnp.concatenate(pieces, axis=0)
    return ((full.T - 1.0) * (-NEG)).astype(BF16)


def _moba_kernel(q_ref, k_ref, v_ref, z_ref, o_ref, qp_ref, kaug_ref, vaug_ref, bias_ref, osc_ref, *, seq):
    t = pl.program_id(1)
    nb = seq // MOBA_BLOCK
    n_pair = MOBA_HEADS // 2

    @pl.when(t == 0)
    def _():
        for p in range(n_pair):
            sl = slice(p * LANES, (p + 1) * LANES)
            qp_ref[p] = q_ref[:, sl]
            kaug_ref[2 * p] = k_ref[:, sl]
            vaug_ref[2 * p] = v_ref[:, sl]
        rowblk = lax.broadcasted_iota(jnp.int32, (seq, LANES), 0) // MOBA_BLOCK
        lane = lax.broadcasted_iota(jnp.int32, (seq, LANES), 1)
        low = lane < HEAD_DIM
        blk = lax.broadcasted_iota(jnp.int32, (nb, seq), 0)
        own = lax.broadcasted_iota(jnp.int32, (nb, seq), 1) // MOBA_BLOCK
        avg = jnp.where(blk == own, 1.0 / MOBA_BLOCK, 0.0).astype(BF16)

        keep0 = jnp.where(low, 1.0, 0.0).astype(BF16)
        keep1 = jnp.where(low, 0.0, 1.0).astype(BF16)
        hot0 = jnp.where(lane - HEAD_DIM == rowblk, 1.0, 0.0).astype(BF16)
        hot1 = jnp.where(lane == rowblk, 1.0, 0.0).astype(BF16)

        def build(p, carry):
            k = kaug_ref[2 * p]
            v = vaug_ref[2 * p]
            kaug_ref[2 * p] = k * keep0 + hot0
            kaug_ref[2 * p + 1] = k * keep1 + hot1
            vaug_ref[2 * p] = v * keep0 + keep1
            vaug_ref[2 * p + 1] = v * keep1 + keep0
            km3 = jnp.concatenate(_split3(_dot(avg, k)), axis=0).astype(BF16)
            q_all = qp_ref[p]
            for h in range(2):
                mine = low if h == 0 else jnp.logical_not(low)
                g3 = _dot_nt(km3, jnp.where(mine, q_all, jnp.zeros((), BF16)))
                gate = g3[:nb] + g3[nb:2 * nb] + g3[2 * nb:]
                gate = jnp.where(blk < own, gate, -jnp.inf)
                cnt = _rank_before(gate, nb)
                allowed = jnp.logical_or(jnp.logical_and(cnt < MOBA_TOPK, blk < own), blk == own)
                bias_ref[2 * p + h] = _bias_lanes(jnp.where(allowed, 1.0, 0.0), HEAD_DIM if h == 0 else 0)
            return carry

        lax.fori_loop(0, n_pair, build, 0)

    lane = lax.broadcasted_iota(jnp.int32, (TQ, LANES), 1)

    for ts in range(nb // MOBA_TILES_PER_STEP):
        @pl.when(t == ts)
        def _(ts=ts):
            def body(it, carry):
                heads = [2 * (it * MOBA_PAIRS_PER_ITER + pp) + h for pp in range(MOBA_PAIRS_PER_ITER) for h in range(2)]
                scs = []
                for sub in range(MOBA_TILES_PER_STEP):
                    tt = ts * MOBA_TILES_PER_STEP + sub
                    rows = slice(tt * TQ, (tt + 1) * TQ)
                    qaugs = []
                    for n, i in enumerate(heads):
                        q = qp_ref[i // 2, rows, :]
                        mine = (lane < HEAD_DIM) if n % 2 == 0 else (lane >= HEAD_DIM)
                        qaugs.append(jnp.where(mine, q, bias_ref[i, rows, :]))
                    scs.append(_prefix_scores(qaugs, [lambda start, size, i=i: kaug_ref[i, pl.ds(start, size), :] for i in heads], tt + 1))
                for sub in range(MOBA_TILES_PER_STEP):
                    tt = ts * MOBA_TILES_PER_STEP + sub
                    outs = _prefix_values(scs[sub], [lambda start, size, i=i: vaug_ref[i, pl.ds(start, size), :] for i in heads],
                                          tt + 1, [HEAD_DIM if n % 2 == 0 else 0 for n in range(len(heads))])
                    for pp in range(MOBA_PAIRS_PER_ITER):
                        osc_ref[it * MOBA_PAIRS_PER_ITER + pp, sub * TQ:(sub + 1) * TQ, :] = jnp.where(lane < HEAD_DIM, outs[2 * pp], outs[2 * pp + 1])
                return carry

            lax.fori_loop(0, n_pair // MOBA_PAIRS_PER_ITER, body, 0)

    gate_z = _silu(z_ref[...].astype(F32))
    for p in range(n_pair):
        sl = slice(p * LANES, (p + 1) * LANES)
        o_ref[:, sl] = (osc_ref[p] * gate_z[:, sl]).astype(BF16)


def _moba(slab, batch, seq):
    rows = TQ * MOBA_TILES_PER_STEP
    nt = seq // rows
    w = MOBA_HEADS * HEAD_DIM
    n_pair = MOBA_HEADS // 2
    plane = lambda n: pltpu.VMEM((n, seq, LANES), BF16)
    return pl.pallas_call(
        functools.partial(_moba_kernel, seq=seq),
        grid=(batch, nt),
        in_specs=[
            pl.BlockSpec((seq, w), lambda b, t: (b, _cb('moba_q', w))),
            pl.BlockSpec((seq, w), lambda b, t: (b, _cb('moba_k', w))),
            pl.BlockSpec((seq, w), lambda b, t: (b, _cb('moba_v', w))),
            pl.BlockSpec((rows, w), lambda b, t: (b * nt + t, _cb('moba_z', w))),
        ],
        out_specs=pl.BlockSpec((rows, w), lambda b, t: (b * nt + t, 0)),
        out_shape=jax.ShapeDtypeStruct((batch * seq, w), BF16),
        scratch_shapes=[
            plane(n_pair), plane(MOBA_HEADS), plane(MOBA_HEADS), plane(MOBA_HEADS),
            pltpu.VMEM((n_pair, rows, LANES), F32),
        ],
        compiler_params=pltpu.CompilerParams(dimension_semantics=("parallel", "arbitrary")),
        name="moba",
    )(slab, slab, slab, slab)


GQA_W = 4 * HEAD_DIM
ONES_A, ONES_B = 3 * HEAD_DIM, 0


def _rep2(x, hk):
    lane = lax.broadcasted_iota(jnp.int32, x.shape, 1)
    keep = (lane < HEAD_DIM) == (hk == 0)
    return jnp.where(keep, x, pltpu.roll(x, HEAD_DIM, 1))


def _value_variants(v2):
    low = lax.broadcasted_iota(jnp.int32, v2.shape, 1) < HEAD_DIM
    va = jnp.concatenate([v2, jnp.where(low, v2, 1.0)], axis=1).astype(BF16)
    vb = jnp.concatenate([jnp.where(low, 1.0, v2), v2], axis=1).astype(BF16)
    return va, vb


def _head_mask(g, width, rows=TQ):
    lane = lax.broadcasted_iota(jnp.int32, (rows, width), 1)
    return (lane // HEAD_DIM) == g


SWA_SPAN = TQ + LANES
SWA_TILES_PER_STEP = 2


def _swa_kernel(sink_ref, q_ref, k_ref, v_ref, z_ref, o_ref, krep_ref, va_ref, vb_ref):
    t = pl.program_id(1)

    @pl.when(t == 0)
    def _():
        for hk in range(SWA_KV_HEADS):
            kr = _rep2(k_ref[...].astype(F32), hk).astype(BF16)
            krep_ref[hk] = jnp.concatenate([kr, kr], axis=1)
            va_ref[hk], vb_ref[hk] = _value_variants(_rep2(v_ref[...].astype(F32), hk))

    gate_z = _silu(z_ref[...].astype(F32))
    masks = [_head_mask(g, GQA_W) for g in range(4)]
    chains = []
    for sub in range(SWA_TILES_PER_STEP):
        r0 = (t * SWA_TILES_PER_STEP + sub) * TQ
        k0 = pl.multiple_of(jnp.maximum(r0 - LANES, 0), LANES)
        dist = (r0 + lax.broadcasted_iota(jnp.int32, (TQ, SWA_SPAN), 0)) - (
            k0 + lax.broadcasted_iota(jnp.int32, (TQ, SWA_SPAN), 1))
        mask = jnp.logical_and(dist >= 0, dist < SWA_WINDOW)
        for hk in range(SWA_KV_HEADS):
            q = q_ref[sub * TQ:(sub + 1) * TQ, hk * GQA_W:(hk + 1) * GQA_W]
            kw = krep_ref[hk, pl.ds(k0, SWA_SPAN), :]
            for g in range(4):
                s = jnp.where(mask, _dot_nt(jnp.where(masks[g], q, jnp.zeros((), BF16)), kw), NEG)
                sink = sink_ref[hk * 4 + g] * LOG2E
                m = jnp.maximum(jnp.max(s, axis=-1, keepdims=True), sink)
                chains.append((sub, hk, g, s, m, sink, k0))
    outs = {}
    for sub, hk, g, s, m, sink, k0 in chains:
        vref, ones = (va_ref, ONES_A) if g < 3 else (vb_ref, ONES_B)
        o = _dot(jnp.exp2(s - m).astype(BF16), vref[hk, pl.ds(k0, SWA_SPAN), :])
        den = o[:, ones:ones + 1] + jnp.exp2(sink - m)
        o = o * (1.0 / jnp.maximum(den, TINY))
        outs[(sub, hk)] = o if g == 0 else jnp.where(masks[g], o, outs[(sub, hk)])
    for (sub, hk), out in outs.items():
        rows, cols = slice(sub * TQ, (sub + 1) * TQ), slice(hk * GQA_W, (hk + 1) * GQA_W)
        o_ref[rows, cols] = (out * gate_z[rows, cols]).astype(BF16)


def _swa(slab, sink, batch, seq):
    rows = TQ * SWA_TILES_PER_STEP
    ns = seq // rows
    w = SWA_HEADS * HEAD_DIM
    plane = pltpu.VMEM((SWA_KV_HEADS, seq, GQA_W), BF16)
    return pl.pallas_call(
        _swa_kernel,
        grid=(batch, ns),
        in_specs=[
            pl.BlockSpec(memory_space=pltpu.SMEM),
            pl.BlockSpec((rows, w), lambda b, t: (b * ns + t, _cb('swa_q', w))),
            pl.BlockSpec((seq, LANES), lambda b, t: (b, _cb('swa_k'))),
            pl.BlockSpec((seq, LANES), lambda b, t: (b, _cb('swa_v'))),
            pl.BlockSpec((rows, w), lambda b, t: (b * ns + t, _cb('swa_z', w))),
        ],
        out_specs=pl.BlockSpec((rows, w), lambda b, t: (b * ns + t, 0)),
        out_shape=jax.ShapeDtypeStruct((batch * seq, w), BF16),
        scratch_shapes=[plane, plane, plane],
        compiler_params=pltpu.CompilerParams(dimension_semantics=("parallel", "arbitrary")),
        name="swa",
    )(sink, slab, slab, slab, slab)


def _ret_kernel(qk_ref, v_ref, z_ref, o_ref, state_ref):
    c = RET_CHUNK

    @pl.when(pl.program_id(1) == 0)
    def _():
        state_ref[...] = jnp.zeros(state_ref.shape, F32)

    q = qk_ref[:, :RET_HEADS * RET_QK_DIM]
    k = qk_ref[:, RET_HEADS * RET_QK_DIM:]
    ii = lax.broadcasted_iota(jnp.int32, (c, c), 0)
    jj = lax.broadcasted_iota(jnp.int32, (c, c), 1)
    diff = (ii - jj).astype(F32)
    row = lax.broadcasted_iota(jnp.int32, (c, 1), 0).astype(F32)
    lane = lax.broadcasted_iota(jnp.int32, (c, RET_HEADS * RET_QK_DIM), 1)
    for h in range(RET_HEADS):
        log_g = math.log(1.0 - 2.0 ** (-5.0 - h))
        intra = jnp.where(diff >= 0, jnp.exp(jnp.maximum(diff, 0.0) * log_g), 0.0)
        q_dec = jnp.exp((row + 1.0) * log_g)
        k_dec = jnp.exp((c - 1.0 - row) * log_g)
        chunk_dec = math.exp(c * log_g)
        qm = jnp.where((lane // RET_QK_DIM) == h, q, jnp.zeros((), BF16))
        vh = v_ref[:, h * RET_V_DIM:(h + 1) * RET_V_DIM]
        att = _dot_nt(qm, k) * intra
        st = state_ref[h]
        o = _dot(att.astype(BF16), vh) + _dot(qm, st.astype(BF16)) * q_dec
        kd = (k.astype(F32) * k_dec).T.astype(BF16)
        state_ref[h] = st * chunk_dec + _dot(kd, vh)
        mu = jnp.mean(o, axis=-1, keepdims=True)
        var = jnp.mean(jnp.square(o - mu), axis=-1, keepdims=True)
        o = (o - mu) * lax.rsqrt(var + NORM_EPS)
        zh = z_ref[:, h * RET_V_DIM:(h + 1) * RET_V_DIM].astype(F32)
        o_ref[:, h * RET_V_DIM:(h + 1) * RET_V_DIM] = (o * _silu(zh)).astype(BF16)


def _retention(slab, batch, seq):
    nc = seq // RET_CHUNK
    w = RET_HEADS * RET_V_DIM
    return pl.pallas_call(
        _ret_kernel,
        grid=(batch, nc),
        in_specs=[
            pl.BlockSpec((RET_CHUNK, w), lambda b, c: (b * nc + c, _cb('ret_q', w))),
            pl.BlockSpec((RET_CHUNK, w), lambda b, c: (b * nc + c, _cb('ret_v', w))),
            pl.BlockSpec((RET_CHUNK, w), lambda b, c: (b * nc + c, _cb('ret_z', w))),
        ],
        out_specs=pl.BlockSpec((RET_CHUNK, w), lambda b, c: (b * nc + c, 0)),
        out_shape=jax.ShapeDtypeStruct((batch * seq, w), BF16),
        scratch_shapes=[pltpu.VMEM((RET_HEADS, RET_HEADS * RET_QK_DIM, RET_V_DIM), F32)],
        compiler_params=pltpu.CompilerParams(dimension_semantics=("parallel", "arbitrary")),
        name="retention",
    )(slab, slab, slab)


def _compress_kernel(r_ref, w1_ref, w2_ref, pos_ref, cos_ref, sa_ref, sb_ref, o_ref):
    kind = pl.program_id(0) // NSA_KV_HEADS
    half = NSA_CMP_STRIDE * HEAD_DIM
    r = r_ref[...]
    n_rows = r.shape[0]
    lo = _dot(r, w1_ref[:half, :])
    hi = _dot(r, w1_ref[half:, :])
    pos = _dot(pos_ref[...], w1_ref[...])[0:1, :]
    pre = lo + pltpu.roll(hi, n_rows - 1, 0) + pos
    y = _dot(jax.nn.gelu(pre).astype(BF16), w2_ref[...])
    roped = _rope_chunk(y, cos_ref[...], sa_ref[...], sb_ref[...])
    o_ref[...] = jnp.where(kind == 0, roped, y)


def _compress(r, w1, w2, pos8, cos_c, sa_c, sb_c, batch):
    n_rows = r.shape[2]
    flat = NSA_CMP_LEN * HEAD_DIM
    return pl.pallas_call(
        _compress_kernel,
        grid=(2 * NSA_KV_HEADS, batch),
        in_specs=[
            pl.BlockSpec((None, None, n_rows, flat // 2), lambda i, b: (i, b, 0, 0)),
            pl.BlockSpec((None, flat, NSA_CMP_HIDDEN), lambda i, b: (i // NSA_KV_HEADS, 0, 0)),
            pl.BlockSpec((None, NSA_CMP_HIDDEN, LANES), lambda i, b: (i // NSA_KV_HEADS, 0, 0)),
            pl.BlockSpec((8, flat), lambda i, b: (0, 0)),
            pl.BlockSpec((n_rows, LANES), lambda i, b: (0, 0)),
            pl.BlockSpec((n_rows, LANES), lambda i, b: (0, 0)),
            pl.BlockSpec((n_rows, LANES), lambda i, b: (0, 0)),
        ],
        out_specs=pl.BlockSpec((None, None, n_rows, LANES), lambda i, b: (i, b, 0, 0)),
        out_shape=jax.ShapeDtypeStruct((2 * NSA_KV_HEADS, batch, n_rows, LANES), F32),
        compiler_params=pltpu.CompilerParams(dimension_semantics=("parallel", "parallel")),
        name="nsa_compress",
    )(r, w1, w2, pos8, cos_c, sa_c, sb_c)


NSA_WIN_SPAN = TQ + NSA_WINDOW
NSA_TILES_PER_STEP = 2
SEL_LANE0 = 64


def _nsa_kernel(q_ref, ks_ref, vs_ref, kw_ref, vw_ref, kc_ref, vc_ref, g_ref, z_ref, o_ref,
                ksa_ref, ksb_ref, vsa_ref, vsb_ref, kwr_ref, vwa_ref, vwb_ref, kcr_ref, vcr_ref,
                ocmp_ref, bias_ref, *, seq):
    hk = pl.program_id(1)
    t = pl.program_id(2)
    n_slc = seq // NSA_SLC_BLOCK
    n_cmp = (seq - NSA_CMP_LEN) // NSA_CMP_STRIDE + 1
    n_crow = kc_ref.shape[0]
    w = GQA_W

    @pl.when(t == 0)
    def _():
        ks = _rep2(ks_ref[...].astype(F32), hk).astype(BF16)
        rowblk = lax.broadcasted_iota(jnp.int32, (seq, LANES), 0) // NSA_SLC_BLOCK
        lane = lax.broadcasted_iota(jnp.int32, (seq, LANES), 1)
        onehot = jnp.where(lane - SEL_LANE0 == rowblk, 1.0, 0.0).astype(BF16)
        ksa_ref[...] = jnp.concatenate([ks, onehot], axis=1)
        ksb_ref[...] = jnp.concatenate([onehot, ks], axis=1)
        vsa_ref[...], vsb_ref[...] = _value_variants(_rep2(vs_ref[...].astype(F32), hk))
        kwin = _rep2(kw_ref[...].astype(F32), hk).astype(BF16)
        kwr_ref[...] = jnp.concatenate([kwin, kwin], axis=1)
        vwa_ref[...], vwb_ref[...] = _value_variants(_rep2(vw_ref[...].astype(F32), hk))
        kc = kc_ref[...]
        kc = (kc + pltpu.roll(kc, HEAD_DIM, 1)).astype(BF16)
        kcr_ref[...] = jnp.concatenate([kc, kc], axis=1)
        vc = vc_ref[...]
        vc = (vc + pltpu.roll(vc, HEAD_DIM, 1)).astype(BF16)
        vcr_ref[...] = jnp.concatenate([vc, vc], axis=1)

        q_all = q_ref[...]
        pos_all = lax.broadcasted_iota(jnp.int32, (seq, 1), 0)
        ci = lax.broadcasted_iota(jnp.int32, (1, n_crow), 1)
        cmask = jnp.logical_and(ci * NSA_CMP_STRIDE + NSA_CMP_LEN - 1 <= pos_all, ci < n_cmp)
        kcr, vcr = kcr_ref[...], vcr_ref[...]
        cmasks = [_head_mask(g, w, seq) for g in range(4)]
        s_cs = [jnp.where(cmask, _dot_nt(jnp.where(cmasks[g], q_all, jnp.zeros((), BF16)), kcr), NEG) for g in range(4)]
        p_cs = []
        for s in s_cs:
            m = jnp.max(s, axis=-1, keepdims=True)
            e = jnp.where(cmask, jnp.exp2(s - m), 0.0)
            p_cs.append(e * (1.0 / jnp.maximum(jnp.sum(e, axis=-1, keepdims=True), TINY)))
        p_sum = (p_cs[0] + p_cs[1]) + (p_cs[2] + p_cs[3])
        o_cs = [_dot(p.astype(BF16), vcr) for p in p_cs]
        o_cmp = o_cs[0]
        for g in range(1, 4):
            o_cmp = jnp.where(cmasks[g], o_cs[g], o_cmp)
        ocmp_ref[...] = o_cmp

        c0 = lax.broadcasted_iota(jnp.int32, (n_slc, n_crow), 1) * NSA_CMP_STRIDE
        s0 = lax.broadcasted_iota(jnp.int32, (n_slc, n_crow), 0) * NSA_SLC_BLOCK
        ov = jnp.minimum(c0 + NSA_CMP_LEN, s0 + NSA_SLC_BLOCK) - jnp.maximum(c0, s0)
        ov = (jnp.maximum(ov, 0).astype(F32) * (1.0 / NSA_CMP_LEN)).astype(BF16)
        imp = _dot_nt(jnp.concatenate([ov, ov, ov], axis=1),
                      jnp.concatenate(_split3(p_sum), axis=1).astype(BF16))
        j_iota = lax.broadcasted_iota(jnp.int32, (n_slc, seq), 0)
        q_blk = lax.broadcasted_iota(jnp.int32, (n_slc, seq), 1) // NSA_SLC_BLOCK
        forced = jnp.logical_or(j_iota == 0, jnp.logical_or(j_iota == q_blk, j_iota == q_blk - 1))
        imp = jnp.where(forced, jnp.inf, imp)
        imp = jnp.where(j_iota <= q_blk, imp, -jnp.inf)
        cnt = _rank_before(imp, n_slc)
        allowed = jnp.logical_and(cnt < min(NSA_SLC_TOPN, n_slc), j_iota <= q_blk)
        bias_ref[...] = _bias_lanes(jnp.where(allowed, 1.0, 0.0), SEL_LANE0)

    zero = jnp.zeros((), BF16)
    masks = [_head_mask(g, w) for g in range(4)]
    er = lax.broadcasted_iota(jnp.int32, (2 * LANES, 3 * w), 0) % LANES
    ec = lax.broadcasted_iota(jnp.int32, (2 * LANES, 3 * w), 1)
    expand = jnp.where(er == (ec // w) * NSA_HEADS + hk * 4 + (ec % w) // HEAD_DIM, 1.0, 0.0).astype(BF16)
    gate_z = _silu(z_ref[...].astype(F32))
    tiles = []
    for sub in range(NSA_TILES_PER_STEP):
        rows = pl.ds(pl.multiple_of((t * NSA_TILES_PER_STEP + sub) * TQ, TQ), TQ)
        q = q_ref[rows, :]
        bias = bias_ref[rows, :]
        qms = [jnp.where(masks[g], q, zero) for g in range(4)]
        qaugs = []
        for g in range(4):
            half_mask = _head_mask(g % 2, LANES)
            if g < 2:
                qaugs.append(jnp.concatenate([jnp.where(half_mask, q[:, :LANES], zero), bias], axis=1))
            else:
                qaugs.append(jnp.concatenate([bias, jnp.where(half_mask, q[:, LANES:], zero)], axis=1))
        gates = jax.nn.sigmoid(g_ref[sub * TQ:(sub + 1) * TQ, :].astype(F32))
        g_hi = gates.astype(BF16)
        g_lo = (gates - g_hi.astype(F32)).astype(BF16)
        gx = _dot(jnp.concatenate([g_hi, g_lo], axis=1), expand)
        tiles.append((qms, qaugs, gx, gx[:, :w] * ocmp_ref[rows, :]))

    k_s = [ksa_ref if g < 2 else ksb_ref for g in range(4)]
    v_w = [(vwa_ref, ONES_A) if g < 3 else (vwb_ref, ONES_B) for g in range(4)]
    v_s = [(vsa_ref, ONES_A) if g < 3 else (vsb_ref, ONES_B) for g in range(4)]
    for ts in range(seq // (TQ * NSA_TILES_PER_STEP)):
        @pl.when(t == ts)
        def _(ts=ts):
            staged = []
            for sub in range(NSA_TILES_PER_STEP):
                tt = ts * NSA_TILES_PER_STEP + sub
                qms, qaugs = tiles[sub][:2]
                k0 = max(tt * TQ - NSA_WINDOW, 0)
                kwin = kwr_ref[k0:k0 + NSA_WIN_SPAN, :]
                dist = (tt * TQ + lax.broadcasted_iota(jnp.int32, (TQ, 1), 0)) - (
                    k0 + lax.broadcasted_iota(jnp.int32, (1, NSA_WIN_SPAN), 1))
                wmask = jnp.logical_and(dist >= 0, dist < NSA_WINDOW)
                s_ws = [jnp.where(wmask, _dot_nt(qms[g], kwin), NEG) for g in range(4)]
                sel_scores = _prefix_scores(qaugs, [lambda start, size, r=r: r[start:start + size, :] for r in k_s], tt + 1)
                m_ws = [jnp.max(s, axis=-1, keepdims=True) for s in s_ws]
                staged.append((tt, k0, s_ws, m_ws, sel_scores))
            for sub, (tt, k0, s_ws, m_ws, sel_scores) in enumerate(staged):
                gx, o_cmp = tiles[sub][2:]
                wins = [_normalize(_dot(jnp.exp2(s_ws[g] - m_ws[g]).astype(BF16), v_w[g][0][k0:k0 + NSA_WIN_SPAN, :]),
                                   v_w[g][1]) for g in range(4)]
                slcs = _prefix_values(sel_scores, [lambda start, size, r=r[0]: r[start:start + size, :] for r in v_s],
                                      tt + 1, [r[1] for r in v_s])
                o_win, o_slc = wins[0], slcs[0]
                for g in range(1, 4):
                    o_win = jnp.where(masks[g], wins[g], o_win)
                    o_slc = jnp.where(masks[g], slcs[g], o_slc)
                o = o_cmp + gx[:, w:2 * w] * o_slc + gx[:, 2 * w:] * o_win
                r = slice(sub * TQ, (sub + 1) * TQ)
                o_ref[r, :] = (o * gate_z[r, :]).astype(BF16)


def _nsa(slab, cmp_kv, batch, seq):
    rows = TQ * NSA_TILES_PER_STEP
    nt = seq // rows
    w = GQA_W
    n_crow = cmp_kv.shape[2]
    kv_spec = lambda name: pl.BlockSpec((seq, LANES), lambda b, h, t: (b, _cb(name)))
    big = pltpu.VMEM((seq, w), BF16)
    return pl.pallas_call(
        functools.partial(_nsa_kernel, seq=seq),
        grid=(batch, NSA_KV_HEADS, nt),
        in_specs=[
            pl.BlockSpec((seq, w), lambda b, h, t: (b, _cb('nsa_q', w) + h)),
            kv_spec('nsa_ks'), kv_spec('nsa_vs'), kv_spec('nsa_kw'), kv_spec('nsa_vw'),
            pl.BlockSpec((None, None, n_crow, LANES), lambda b, h, t: (h, b, 0, 0)),
            pl.BlockSpec((None, None, n_crow, LANES), lambda b, h, t: (NSA_KV_HEADS + h, b, 0, 0)),
            pl.BlockSpec((rows, LANES), lambda b, h, t: (b * nt + t, _cb('nsa_g'))),
            pl.BlockSpec((rows, w), lambda b, h, t: (b * nt + t, _cb('nsa_z', w) + h)),
        ],
        out_specs=pl.BlockSpec((rows, w), lambda b, h, t: (b * nt + t, h)),
        out_shape=jax.ShapeDtypeStruct((batch * seq, NSA_HEADS * HEAD_DIM), BF16),
        scratch_shapes=[
            big, big, big, big, big, big, big,
            pltpu.VMEM((n_crow, w), BF16), pltpu.VMEM((n_crow, w), BF16),
            pltpu.VMEM((seq, w), F32), pltpu.VMEM((seq, LANES), BF16),
        ],
        compiler_params=pltpu.CompilerParams(dimension_semantics=("parallel", "parallel", "arbitrary")),
        name="nsa",
    )(slab, slab, slab, slab, slab, cmp_kv, cmp_kv, slab, slab)


def _merge_kernel(x_ref, g_ref, oa_ref, ob_ref, oc_ref, od_ref, wb_ref, wo_ref, fg_ref, o_ref, *, final):
    merged = None
    row = 0
    for i, br in enumerate((oa_ref, ob_ref, oc_ref, od_ref)):
        width = br.shape[1]
        y = _dot(br[...], wb_ref[row:row + width, :])
        gate = jax.nn.sigmoid(g_ref[:, i * D_MODEL:(i + 1) * D_MODEL].astype(F32))
        merged = gate * y if merged is None else merged + gate * y
        row += width
    x = x_ref[...] + _dot(merged.astype(BF16), wo_ref[...])
    if final:
        ms = jnp.mean(x * x, axis=-1, keepdims=True)
        x = x * lax.rsqrt(ms + NORM_EPS) * fg_ref[...]
    o_ref[...] = x


def _merge(x2, slab, outs, w_branch, w_out, final_g, final):
    t = x2.shape[0]
    tm = MERGE_TM
    d_branch = w_branch.shape[0]
    row_spec = lambda width: pl.BlockSpec((tm, width), lambda i: (i, 0))
    return pl.pallas_call(
        functools.partial(_merge_kernel, final=final),
        grid=(t // tm,),
        in_specs=[
            row_spec(D_MODEL),
            pl.BlockSpec((tm, N_BRANCH * D_MODEL), lambda i: (i, _cb('merge_g', N_BRANCH * D_MODEL))),
            row_spec(outs[0].shape[1]), row_spec(outs[1].shape[1]),
            row_spec(outs[2].shape[1]), row_spec(outs[3].shape[1]),
            pl.BlockSpec((d_branch, D_MODEL), lambda i: (0, 0)),
            pl.BlockSpec((D_MODEL, D_MODEL), lambda i: (0, 0)),
            pl.BlockSpec((1, D_MODEL), lambda i: (0, 0)),
        ],
        out_specs=row_spec(D_MODEL),
        out_shape=jax.ShapeDtypeStruct((t, D_MODEL), F32),
        compiler_params=pltpu.CompilerParams(dimension_semantics=("parallel",)),
        name="merge_out",
    )(x2, slab, *outs, w_branch, w_out, final_g.reshape(1, D_MODEL))


def _rope_tables(positions):
    inv = 1.0 / (ROPE_THETA ** (jnp.arange(0, HEAD_DIM, 2, dtype=F32) / HEAD_DIM))
    ang = positions.astype(F32)[:, None] * inv[None, :]
    cos, sin = jnp.cos(ang), jnp.sin(ang)
    zero = jnp.zeros_like(sin)
    cos_t = jnp.tile(cos, (1, 4))
    sa_t = jnp.tile(jnp.concatenate([-sin, zero], axis=1), (1, 2))
    sb_t = jnp.tile(jnp.concatenate([zero, sin], axis=1), (1, 2))
    return cos_t, sa_t, sb_t


def kernel(x, norm_g, w_in, w_branch, w_out, swa_sink, nsa_cmp_pos, nsa_w_ck1, nsa_w_ck2,
           nsa_w_cv1, nsa_w_cv2, final_norm_g):
    batch, seq, d = x.shape
    depth = w_in.shape[0]
    assert d == D_MODEL and seq % TQ == 0 and seq % RET_CHUNK == 0 and (batch * seq) % MERGE_TM == 0
    assert seq >= NSA_WIN_SPAN and seq % (8 * NSA_CMP_STRIDE) == 0 and (seq // MOBA_BLOCK) % 8 == 0

    cos_t, sa_t, sb_t = _rope_tables(jnp.arange(seq))
    n_crow = seq // NSA_CMP_STRIDE
    cos_c, sa_c, sb_c = _rope_tables(jnp.arange(n_crow) * NSA_CMP_STRIDE + NSA_CMP_LEN - 1)

    x2 = x.reshape(batch * seq, d)
    for l in range(depth):
        slab, kcvc = _proj_in(x2, norm_g[l], _slab_weights(w_in[l]), cos_t, sa_t, sb_t, seq)

        groups = kcvc.reshape(2 * NSA_KV_HEADS, batch, n_crow, NSA_CMP_STRIDE * HEAD_DIM)
        w1 = jnp.stack([nsa_w_ck1[l], nsa_w_cv1[l]]).astype(BF16)
        w2 = jnp.pad(jnp.stack([nsa_w_ck2[l], nsa_w_cv2[l]]), ((0, 0), (0, 0), (0, LANES - HEAD_DIM))).astype(BF16)
        pos8 = jnp.broadcast_to(nsa_cmp_pos[l].reshape(1, -1), (8, NSA_CMP_LEN * HEAD_DIM)).astype(BF16)
        cmp_kv = _compress(groups, w1, w2, pos8, cos_c, sa_c, sb_c, batch)

        outs = (
            _moba(slab, batch, seq),
            _swa(slab, swa_sink[l], batch, seq),
            _retention(slab, batch, seq),
            _nsa(slab, cmp_kv, batch, seq),
        )
        x2 = _merge(x2, slab, outs, w_branch[l].astype(BF16), w_out[l].astype(BF16),
                    final_norm_g, final=(l == depth - 1))
    return x2.reshape(batch, seq, d)
```
